```python
import math
import jax, jax.numpy as jnp
from jax import lax
import numpy as np

D_MODEL = 1024
BATCH = 4
SEQ = 4096
DEPTH = 2
DEC_BATCH = 128
DEC_SEQ = 8
PAST_LEN = 2048
PAGE_SIZE = 128

D_MIX = D_MODEL
D_A = D_MIX // 2
H_A = 8
DH_A = D_A // H_A
CHUNK = 128
D_B = D_MIX - D_A
H_B = 4
DV = D_B // H_B
DK = DV // 2
D_QK = H_B * 2 * DK
D_IN = 2 * D_A + 2 * D_QK + D_B
Q_BLOCK = 128
ATTN_SCALE = DK ** -0.5
NEG_INF = -1e30
N_GROUPS = 4
E_PER = 4
N_EXP = N_GROUPS * E_PER
TOP_K = 2
D_E = 256
EPS = 1e-6

kernel_name = "hymba_chunkmlp_diffattn_hmoe_step"


def rmsnorm(x, g):
    xf = x.astype(jnp.float32)
    y = xf * lax.rsqrt(jnp.mean(xf * xf, axis=-1, keepdims=True) + EPS)
    return y.astype(x.dtype) * g


def in_proj(x, g1, w_in, av_g, qn_g, kn_g):
    b, t, _ = x.shape
    z = rmsnorm(x, g1) @ w_in
    a = jax.nn.gelu(z[..., :2 * D_A])
    u = a[..., :D_A]
    va = rmsnorm(a[..., D_A:], av_g)
    q = z[..., 2 * D_A:2 * D_A + D_QK].reshape(b, t, H_B, 2, DK)
    k = z[..., 2 * D_A + D_QK:2 * D_A + 2 * D_QK].reshape(b, t, H_B, 2, DK)
    vb = z[..., 2 * D_A + 2 * D_QK:].reshape(b, t, H_B, DV)
    return u, va, rmsnorm(q, qn_g), rmsnorm(k, kn_g), vb


def masked_ws(ws):
    tril = jnp.tril(jnp.ones((CHUNK, CHUNK), dtype=bool))
    return jnp.where(tril[None], ws, jnp.zeros_like(ws))


def chunk_mix_prompt(va, ws_m, bs):
    b, t, _ = va.shape
    vc = va.reshape(b, t // CHUNK, CHUNK, H_A, DH_A)
    s = jnp.einsum('hts,bcshd->bcthd', ws_m, vc) + bs.T[None, None, :, :, None]
    return s.reshape(b, t, D_A)


def chunk_mix_sample(va, ws_m, bs):
    b, n, _ = va.shape
    vs = va.reshape(b, n, H_A, DH_A)
    s = jnp.einsum('hts,bshd->bthd', ws_m[:, :n, :n], vs) + bs[:, :n].T[None, :, :, None]
    return s.reshape(b, n, D_A)


def diff_lambda(lq1, lk1, lq2, lk2, lam_init):
    f = jnp.float32
    return (jnp.exp(jnp.sum(lq1.astype(f) * lk1.astype(f))) -
            jnp.exp(jnp.sum(lq2.astype(f) * lk2.astype(f))) + lam_init)


def diff_core(q, k, v, mask, lam):
    s = jnp.einsum('bqhmd,bkhmd->bhmqk', q, k).astype(jnp.float32) * ATTN_SCALE
    s = jnp.where(mask[None, None, None], s, NEG_INF)
    p = jax.nn.softmax(s, axis=-1)
    a = p[:, :, 0] - lam * p[:, :, 1]
    return jnp.einsum('bhqk,bkhd->bqhd', a.astype(v.dtype), v)


def attn_prompt(q, k, v, lam):
    b, t = q.shape[:2]
    kpos = jnp.arange(t)

    def blk(i):
        qs = lax.dynamic_slice_in_dim(q, i * Q_BLOCK, Q_BLOCK, axis=1)
        qpos = i * Q_BLOCK + jnp.arange(Q_BLOCK)
        return diff_core(qs, k, v, kpos[None, :] <= qpos[:, None], lam)

    o = lax.map(blk, jnp.arange(t // Q_BLOCK))
    return jnp.moveaxis(o, 0, 1).reshape(b, t, H_B, DV)


def attn_sample(q, k_new, v_new, k_past, v_past, lam):
    n = q.shape[1]
    past = k_past.shape[1]
    k_all = jnp.concatenate([k_past, k_new], axis=1)
    v_all = jnp.concatenate([v_past, v_new], axis=1)
    mask = jnp.concatenate([jnp.ones((n, past), dtype=bool),
                            jnp.tril(jnp.ones((n, n), dtype=bool))], axis=1)
    return diff_core(q, k_all, v_all, mask, lam)


def merge_out(u, s_a, o_b, on_g, lam_init, w_out):
    b, t, _ = u.shape
    a_out = u * s_a
    b_out = (rmsnorm(o_b, on_g) * (1.0 - lam_init)).reshape(b, t, D_B)
    return jnp.concatenate([a_out, b_out], axis=-1) @ w_out


def hier_moe(x, g2, r_g, r_e, w_gate, w_up, w_down):
    b, t, d = x.shape
    h = rmsnorm(x, g2).reshape(b * t, d)
    pg = jax.nn.softmax((h @ r_g).astype(jnp.float32), axis=-1)
    gi = jnp.argmax(pg, axis=-1)
    gw = jnp.max(pg, axis=-1)
    le = (h @ r_e).astype(jnp.float32).reshape(-1, N_GROUPS, E_PER)
    le_sel = jnp.take_along_axis(le, gi[:, None, None], axis=1)[:, 0]
    tw, ti = lax.top_k(jax.nn.softmax(le_sel, axis=-1), TOP_K)
    tw = tw / jnp.sum(tw, axis=-1, keepdims=True)
    eidx = gi[:, None] * E_PER + ti
    gates = jnp.sum(jax.nn.one_hot(eidx, N_EXP, dtype=jnp.float32) * (gw[:, None] * tw)[..., None], axis=1)
    hg = jnp.einsum('nd,edf->nef', h, w_gate)
    hu = jnp.einsum('nd,edf->nef', h, w_up)
    a = jax.nn.silu(hg) * hu * gates.astype(h.dtype)[..., None]
    return jnp.einsum('nef,efd->nd', a, w_down).reshape(b, t, d)


def setup_inputs(seed: int = 0) -> dict:
    key = jax.random.key(seed)
    ks = jax.random.split(key, 26)
    n_pages = PAST_LEN // PAGE_SIZE
    n_used = DEC_BATCH * n_pages
    n_pool = n_used + n_used // 4
    f = jnp.float32
    nrm = lambda k, shape, s: jax.random.normal(k, shape, f) * s
    gain = lambda k, shape: 1.0 + 0.02 * jax.random.normal(k, shape, f)
    page_table = jax.random.permutation(ks[4], n_pool)[:n_used].reshape(DEC_BATCH, n_pages).astype(jnp.int32)
    return {
        "x_prompt": nrm(ks[0], (BATCH, SEQ, D_MODEL), 1.0),
        "x_sample": nrm(ks[1], (DEC_BATCH, DEC_SEQ, D_MODEL), 1.0),
        "cache_k": nrm(ks[2], (DEPTH, n_pool, PAGE_SIZE, H_B, 2 * DK), 1.0),
        "cache_v": nrm(ks[3], (DEPTH, n_pool, PAGE_SIZE, H_B, DV), 1.0),
        "page_table": page_table,
        "norm1_g": gain(ks[5], (DEPTH, D_MODEL)),
        "w_in": nrm(ks[6], (DEPTH, D_MODEL, D_IN), D_MODEL ** -0.5),
        "a_vnorm_g": gain(ks[7], (DEPTH, D_A)),
        "a_ws": nrm(ks[8], (DEPTH, H_A, CHUNK, CHUNK), CHUNK ** -0.5),
        "a_bs": 1.0 + nrm(ks[9], (DEPTH, H_A, CHUNK), 0.02),
        "qn_g": gain(ks[10], (DEPTH, DK)),
        "kn_g": gain(ks[11], (DEPTH, DK)),
        "lam_q1": nrm(ks[12], (DEPTH, DK), 0.1),
        "lam_k1": nrm(ks[13], (DEPTH, DK), 0.1),
        "lam_q2": nrm(ks[14], (DEPTH, DK), 0.1),
        "lam_k2": nrm(ks[15], (DEPTH, DK), 0.1),
        "b_onorm_g": gain(ks[16], (DEPTH, DV)),
        "w_out": nrm(ks[17], (DEPTH, D_MIX, D_MODEL), D_MIX ** -0.5),
        "norm2_g": gain(ks[18], (DEPTH, D_MODEL)),
        "router_g": nrm(ks[19], (DEPTH, D_MODEL, N_GROUPS), D_MODEL ** -0.5),
        "router_e": nrm(ks[20], (DEPTH, D_MODEL, N_EXP), D_MODEL ** -0.5),
        "w_gate": nrm(ks[21], (DEPTH, N_EXP, D_MODEL, D_E), D_MODEL ** -0.5),
        "w_up": nrm(ks[22], (DEPTH, N_EXP, D_MODEL, D_E), D_MODEL ** -0.5),
        "w_down": nrm(ks[23], (DEPTH, N_EXP, D_E, D_MODEL), D_E ** -0.5),
    }


def reference(x_prompt, x_sample, cache_k, cache_v, page_table, norm1_g, w_in, a_vnorm_g, a_ws, a_bs,
              qn_g, kn_g, lam_q1, lam_k1, lam_q2, lam_k2, b_onorm_g, w_out, norm2_g,
              router_g, router_e, w_gate, w_up, w_down):
    xp, xs = x_prompt, x_sample
    db, n_new = xs.shape[:2]
    past_len = page_table.shape[1] * cache_k.shape[2]
    kp_rows, vp_rows, ks_rows, vs_rows, cv_rows = [], [], [], [], []
    for l in range(DEPTH):
        lam_init = 0.8 - 0.6 * math.exp(-0.3 * l)
        lam = diff_lambda(lam_q1[l], lam_k1[l], lam_q2[l], lam_k2[l], lam_init)
        ws_m = masked_ws(a_ws[l])
        u, va, q, k, vb = in_proj(xp, norm1_g[l], w_in[l], a_vnorm_g[l], qn_g[l], kn_g[l])
        s_a = chunk_mix_prompt(va, ws_m, a_bs[l])
        o_b = attn_prompt(q, k, vb, lam)
        xp = xp + merge_out(u, s_a, o_b, b_onorm_g[l], lam_init, w_out[l])
        xp = xp + hier_moe(xp, norm2_g[l], router_g[l], router_e[l], w_gate[l], w_up[l], w_down[l])
        kp_rows.append(k.reshape(k.shape[0], k.shape[1], H_B, 2 * DK))
        vp_rows.append(vb)
        u, va, q, k, vb = in_proj(xs, norm1_g[l], w_in[l], a_vnorm_g[l], qn_g[l], kn_g[l])
        s_a = chunk_mix_sample(va, ws_m, a_bs[l])
        k_past = cache_k[l][page_table].reshape(db, past_len, H_B, 2, DK)
        v_past = cache_v[l][page_table].reshape(db, past_len, H_B, DV)
        o_b = attn_sample(q, k, vb, k_past, v_past, lam)
        xs = xs + merge_out(u, s_a, o_b, b_onorm_g[l], lam_init, w_out[l])
        xs = xs + hier_moe(xs, norm2_g[l], router_g[l], router_e[l], w_gate[l], w_up[l], w_down[l])
        ks_rows.append(k.reshape(db, n_new, H_B, 2 * DK))
        vs_rows.append(vb)
        cv_rows.append(va)
    k_prompt = jnp.stack(kp_rows)
    v_prompt = jnp.stack(vp_rows)
    k_sample = jnp.stack(ks_rows)
    v_sample = jnp.stack(vs_rows)
    chunk_v_sample = jnp.stack(cv_rows)
    return (xp, xs, k_prompt, v_prompt, k_sample, v_sample, chunk_v_sample)
```

```python
import functools
import math

import jax
import jax.numpy as jnp
from jax import lax
from jax.experimental import pallas as pl
from jax.experimental.pallas import tpu as pltpu

F32 = jnp.float32
BF16 = jnp.bfloat16

D_MODEL = 1024
DEPTH = 2
D_A = 512
H_A = 8
DH_A = D_A // H_A
CHUNK = 128
D_B = 512
H_B = 4
DV = D_B // H_B
DK = DV // 2
D_QK = H_B * 2 * DK
D_IN = 2 * D_A + 2 * D_QK + D_B
ATTN_SCALE = DK ** -0.5
NEG_INF = -1e30
N_GROUPS = 4
E_PER = 4
N_EXP = N_GROUPS * E_PER
D_E = 256
EPS = 1e-6

LANES = 128
MXU_DIM = 256
VMEM_LIMIT_BYTES = 48 * 1024 * 1024

TM_PROJ = 512
TQ = 512
TK = 256
TM_MOE = 1024
ROUTER_LANES = 128
GROUP_LANE0 = N_EXP


def _params(sem):
    return pltpu.CompilerParams(dimension_semantics=sem, vmem_limit_bytes=VMEM_LIMIT_BYTES)


def _const_spec(shape):
    nd = len(shape)
    return pl.BlockSpec(shape, lambda *_: (0,) * nd)


def _rms(x):
    return x * lax.rsqrt(jnp.mean(x * x, axis=-1, keepdims=True) + EPS)


def _dot(a, b):
    return jnp.dot(a, b, preferred_element_type=F32)


def _dot_nt(a, b):
    return lax.dot_general(a, b, (((1,), (1,)), ((), ())), preferred_element_type=F32)


def _in_proj_kernel(x_ref, g1_ref, w_ref, avg_ref, qg_ref, kg_ref, ws_ref, bias_ref, bd_ref,
                    aout_ref, q_ref, k32_ref, kb_ref, v32_ref, vb_ref, va_ref, *, tm):
    x = x_ref[...]
    xb = (_rms(x) * g1_ref[...]).astype(BF16)

    a = jax.nn.gelu(_dot(xb, w_ref[:, 0:2 * D_A]))
    u = a[:, :D_A]
    va = _rms(a[:, D_A:]) * avg_ref[...]
    va_ref[...] = va
    vab = va.astype(BF16)

    row = lax.broadcasted_iota(jnp.int32, (CHUNK, CHUNK), 0)
    col = lax.broadcasted_iota(jnp.int32, (CHUNK, CHUNK), 1)
    tril = row >= col
    wsm = [jnp.where(tril, ws_ref[h], 0.0).astype(BF16) for h in range(H_A)]
    first_head = lax.broadcasted_iota(jnp.int32, (CHUNK, LANES), 1) < DH_A
    for c in range(tm // CHUNK):
        rows = slice(c * CHUNK, (c + 1) * CHUNK)
        for hp in range(H_A // 2):
            cols = slice(hp * LANES, (hp + 1) * LANES)
            blk = vab[rows, cols]
            s = jnp.where(first_head, _dot(wsm[2 * hp], blk), _dot(wsm[2 * hp + 1], blk))
            s = s + bias_ref[:, cols]
            aout_ref[rows, cols] = (u[rows, cols] * s).astype(BF16)

    bd = bd_ref[...]

    def group_norm(z, g):
        sq = (z * z).astype(BF16)
        ss = jnp.concatenate([_dot(sq[:, i * MXU_DIM:(i + 1) * MXU_DIM], bd) for i in range(D_QK // MXU_DIM)], axis=1)
        return z * lax.rsqrt(ss * (1.0 / DK) + EPS) * g

    qn = group_norm(_dot(xb, w_ref[:, 2 * D_A:2 * D_A + D_QK]), qg_ref[...])
    q_ref[...] = (qn * ATTN_SCALE).astype(BF16)
    kn = group_norm(_dot(xb, w_ref[:, 2 * D_A + D_QK:2 * D_A + 2 * D_QK]), kg_ref[...])
    k32_ref[...] = kn
    kb_ref[...] = kn.astype(BF16)
    v = _dot(xb, w_ref[:, 2 * D_A + 2 * D_QK:D_IN])
    v32_ref[...] = v
    vb_ref[...] = v.astype(BF16)


def _in_proj(x2d, g1, w_in_b, avg, qg, kg, ws, bias, bd):
    n = x2d.shape[0]
    tm = TM_PROJ
    row_spec = lambda w: pl.BlockSpec((tm, w), lambda i: (i, 0))
    out_shapes = (
        jax.ShapeDtypeStruct((n, D_A), BF16),
        jax.ShapeDtypeStruct((n, D_QK), BF16),
        jax.ShapeDtypeStruct((n, D_QK), F32),
        jax.ShapeDtypeStruct((n, D_QK), BF16),
        jax.ShapeDtypeStruct((n, D_B), F32),
        jax.ShapeDtypeStruct((n, D_B), BF16),
        jax.ShapeDtypeStruct((n, D_A), F32),
    )
    return pl.pallas_call(
        functools.partial(_in_proj_kernel, tm=tm),
        grid=(n // tm,),
        in_specs=[
            row_spec(D_MODEL),
            _const_spec((1, D_MODEL)),
            _const_spec((D_MODEL, D_IN)),
            _const_spec((1, D_A)),
            _const_spec((1, D_QK)),
            _const_spec((1, D_QK)),
            _const_spec((H_A, CHUNK, CHUNK)),
            _const_spec((CHUNK, D_A)),
            _const_spec((MXU_DIM, MXU_DIM)),
        ],
        out_specs=tuple(row_spec(s.shape[1]) for s in out_shapes),
        out_shape=out_shapes,
        compiler_params=_params(("parallel",)),
        name="in_proj",
    )(x2d, g1, w_in_b, avg, qg, kg, ws, bias, bd)


def _lam(lq1_ref, lk1_ref, lq2_ref, lk2_ref, lam_init):
    s1 = jnp.sum(lq1_ref[...] * lk1_ref[...], axis=-1, keepdims=True)
    s2 = jnp.sum(lq2_ref[...] * lk2_ref[...], axis=-1, keepdims=True)
    return jnp.exp(s1) - jnp.exp(s2) + lam_init


def _softmax_update(s, v, m_ref, l_ref, acc_ref):
    m_prev = m_ref[...]
    m_new = jnp.maximum(m_prev, jnp.max(s, axis=-1, keepdims=True))
    alpha = jnp.exp(m_prev - m_new)
    p = jnp.exp(s - m_new[:, :1])
    l_ref[...] = alpha * l_ref[...] + jnp.sum(p, axis=-1, keepdims=True)
    acc_ref[...] = alpha * acc_ref[...] + _dot(p.astype(BF16), v)
    m_ref[...] = m_new


def _out_norm(o, og, lam_init):
    return _rms(o) * og * (1.0 - lam_init)


def _attn_prompt_kernel(q_ref, k_ref, v_ref, lq1_ref, lk1_ref, lq2_ref, lk2_ref, og_ref, o_ref,
                        m_ref, l_ref, acc_ref, *, lam_init):
    qi = pl.program_id(2)
    q = q_ref[...]
    lane = lax.broadcasted_iota(jnp.int32, (TQ, LANES), 1)
    zero = jnp.zeros_like(q)
    qs = jnp.concatenate([jnp.where(lane < DK, q, zero), jnp.where(lane >= DK, q, zero)], axis=0)

    m_ref[...] = jnp.full(m_ref.shape, NEG_INF, F32)
    l_ref[...] = jnp.zeros(l_ref.shape, F32)
    acc_ref[...] = jnp.zeros(acc_ref.shape, F32)

    def step(kj, masked):
        start = pl.multiple_of(kj * TK, TK)
        k = k_ref[pl.ds(start, TK), :]
        v = v_ref[pl.ds(start, TK), :]
        s = _dot_nt(qs, k)
        if masked:
            r = lax.broadcasted_iota(jnp.int32, (2 * TQ, TK), 0)
            qpos = qi * TQ + jnp.where(r >= TQ, r - TQ, r)
            kpos = kj * TK + lax.broadcasted_iota(jnp.int32, (2 * TQ, TK), 1)
            s = jnp.where(kpos <= qpos, s, NEG_INF)
        _softmax_update(s, v, m_ref, l_ref, acc_ref)

    n_full = qi * (TQ // TK)

    def body(kj, carry):
        step(kj, False)
        return carry

    lax.fori_loop(0, n_full, body, 0)
    for d in range(TQ // TK):
        step(n_full + d, True)

    lam = _lam(lq1_ref, lk1_ref, lq2_ref, lk2_ref, lam_init)
    o = acc_ref[...] / l_ref[...]
    o = o[:TQ] - lam * o[TQ:]
    o_ref[...] = _out_norm(o, og_ref[...], lam_init).astype(BF16)


def _attn_prompt(q, kb, vb, lq1, lk1, lq2, lk2, og, lam_init, batch, seq):
    q3 = q.reshape(batch, seq, D_QK)
    k3 = kb.reshape(batch, seq, D_QK)
    v3 = vb.reshape(batch, seq, D_B)
    vec = _const_spec((1, DK))
    out = pl.pallas_call(
        functools.partial(_attn_prompt_kernel, lam_init=lam_init),
        grid=(batch, H_B, seq // TQ),
        in_specs=[
            pl.BlockSpec((None, TQ, LANES), lambda b, h, i: (b, i, h)),
            pl.BlockSpec((None, seq, LANES), lambda b, h, i: (b, 0, h)),
            pl.BlockSpec((None, seq, LANES), lambda b, h, i: (b, 0, h)),
            vec, vec, vec, vec,
            _const_spec((1, DV)),
        ],
        out_specs=pl.BlockSpec((None, TQ, LANES), lambda b, h, i: (b, i, h)),
        out_shape=jax.ShapeDtypeStruct((batch, seq, D_B), BF16),
        scratch_shapes=[
            pltpu.VMEM((2 * TQ, LANES), F32),
            pltpu.VMEM((2 * TQ, LANES), F32),
            pltpu.VMEM((2 * TQ, DV), F32),
        ],
        compiler_params=_params(("parallel", "parallel", "arbitrary")),
        name="attn_prompt",
    )(q3, k3, v3, lq1, lk1, lq2, lk2, og)
    return out.reshape(batch * seq, D_B)


def _attn_sample_kernel(pt_ref, q_ref, kn_ref, vn_ref, lq1_ref, lk1_ref, lq2_ref, lk2_ref, og_ref, *rest,
                        n_pages, n_new, lam_init):
    k_pages = rest[:n_pages]
    v_pages = rest[n_pages:2 * n_pages]
    o_ref, m_ref, l_ref, acc_ref = rest[2 * n_pages:]
    n_rows = H_B * 2 * n_new

    q = q_ref[...].astype(F32)
    lane = lax.broadcasted_iota(jnp.int32, (n_new, LANES), 1)
    parts = []
    for h in range(H_B):
        qh = q[:, h * LANES:(h + 1) * LANES]
        parts.append(jnp.where(lane < DK, qh, 0.0))
        parts.append(jnp.where(lane >= DK, qh, 0.0))
    qs = jnp.concatenate(parts, axis=0).astype(BF16)

    m_ref[...] = jnp.full(m_ref.shape, NEG_INF, F32)
    l_ref[...] = jnp.zeros(l_ref.shape, F32)
    acc_ref[...] = jnp.zeros(acc_ref.shape, F32)

    def block(k2d, v2d, causal):
        n_cols = k2d.shape[0]
        s = _dot_nt(qs, k2d)
        r = lax.broadcasted_iota(jnp.int32, (n_rows, n_cols), 0)
        c = lax.broadcasted_iota(jnp.int32, (n_rows, n_cols), 1)
        valid = (c & (H_B - 1)) == lax.shift_right_logical(r, int(math.log2(2 * n_new)))
        if causal:
            valid = valid & (lax.shift_right_logical(c, int(math.log2(H_B))) <= (r & (n_new - 1)))
        s = jnp.where(valid, s, NEG_INF)
        _softmax_update(s, v2d, m_ref, l_ref, acc_ref)

    for p in range(n_pages):
        block(k_pages[p][...].astype(BF16), v_pages[p][...].astype(BF16), False)
    block(kn_ref[...], vn_ref[...], True)

    lam = _lam(lq1_ref, lk1_ref, lq2_ref, lk2_ref, lam_init)
    o = acc_ref[...] / l_ref[...]
    for h in range(H_B):
        base = h * 2 * n_new
        oh = o[base:base + n_new] - lam * o[base + n_new:base + 2 * n_new]
        o_ref[:, h * DV:(h + 1) * DV] = _out_norm(oh, og_ref[...], lam_init)


def _attn_sample(q, kb, vb, cache_k, cache_v, page_table, layer, lq1, lk1, lq2, lk2, og, lam_init, dec_batch, n_new):
    n_pages = page_table.shape[1]
    page = cache_k.shape[2]
    n_pool = cache_k.shape[1]
    ck = cache_k.reshape(DEPTH, n_pool, page * H_B, 2 * DK)
    cv = cache_v.reshape(DEPTH, n_pool, page * H_B, DV)
    q3 = q.reshape(dec_batch, n_new, D_QK)
    kn = kb.reshape(dec_batch, n_new * H_B, 2 * DK)
    vn = vb.reshape(dec_batch, n_new * H_B, DV)
    n_rows = H_B * 2 * n_new
    vec = pl.BlockSpec((1, DK), lambda b, pt: (0, 0))

    def page_spec(p):
        return pl.BlockSpec((None, None, page * H_B, LANES), lambda b, pt: (layer, pt[b * n_pages + p], 0, 0))

    grid_spec = pltpu.PrefetchScalarGridSpec(
        num_scalar_prefetch=1,
        grid=(dec_batch,),
        in_specs=[
            pl.BlockSpec((None, n_new, D_QK), lambda b, pt: (b, 0, 0)),
            pl.BlockSpec((None, n_new * H_B, LANES), lambda b, pt: (b, 0, 0)),
            pl.BlockSpec((None, n_new * H_B, LANES), lambda b, pt: (b, 0, 0)),
            vec, vec, vec, vec,
            pl.BlockSpec((1, DV), lambda b, pt: (0, 0)),
        ] + [page_spec(p) for p in range(n_pages)] * 2,
        out_specs=pl.BlockSpec((None, n_new, D_B), lambda b, pt: (b, 0, 0)),
        scratch_shapes=[
            pltpu.VMEM((n_rows, LANES), F32),
            pltpu.VMEM((n_rows, LANES), F32),
            pltpu.VMEM((n_rows, DV), F32),
        ],
    )
    out = pl.pallas_call(
        functools.partial(_attn_sample_kernel, n_pages=n_pages, n_new=n_new, lam_init=lam_init),
        grid_spec=grid_spec,
        out_shape=jax.ShapeDtypeStruct((dec_batch, n_new, D_B), F32),
        compiler_params=_params(("arbitrary",)),
        name="attn_sample",
    )(page_table.reshape(-1), q3, kn, vn, lq1, lk1, lq2, lk2, og, *([ck] * n_pages), *([cv] * n_pages))
    return out.reshape(dec_batch * n_new, D_B)


def _merge_router_kernel(a_ref, b_ref, x_ref, wo_ref, g2_ref, rhi_ref, rlo_ref, x1_ref, h_ref, gates_ref):
    y = _dot(a_ref[...].astype(BF16), wo_ref[0:D_A, :]) + _dot(b_ref[...].astype(BF16), wo_ref[D_A:, :])
    x1 = x_ref[...] + y
    x1_ref[...] = x1
    h = _rms(x1) * g2_ref[...]
    hb = h.astype(BF16)
    h_ref[...] = hb
    hlo = (h - hb.astype(F32)).astype(BF16)
    rhi = rhi_ref[...]
    logits = _dot(hb, rhi) + _dot(hlo, rhi) + _dot(hb, rlo_ref[...])

    lane = lax.broadcasted_iota(jnp.int32, logits.shape, 1)
    big = jnp.int32(ROUTER_LANES)

    def first_lane(mask):
        return jnp.min(jnp.where(mask, lane, big), axis=-1, keepdims=True)

    is_g = (lane >= GROUP_LANE0) & (lane < GROUP_LANE0 + N_GROUPS)
    lg = jnp.where(is_g, logits, NEG_INF)
    eg = jnp.where(is_g, jnp.exp(lg - jnp.max(lg, axis=-1, keepdims=True)), 0.0)
    pg = eg / jnp.sum(eg, axis=-1, keepdims=True)
    gw = jnp.max(pg, axis=-1, keepdims=True)
    gi = first_lane(is_g & (pg == gw)) - GROUP_LANE0
    sel = (lane < N_EXP) & (lax.shift_right_logical(lane, int(math.log2(E_PER))) == gi)
    le = jnp.where(sel, logits, NEG_INF)
    ee = jnp.where(sel, jnp.exp(le - jnp.max(le, axis=-1, keepdims=True)), 0.0)
    pe = ee / jnp.sum(ee, axis=-1, keepdims=True)
    t1 = jnp.max(jnp.where(sel, pe, -1.0), axis=-1, keepdims=True)
    i1 = first_lane(sel & (pe == t1))
    rest = sel & (lane != i1)
    t2 = jnp.max(jnp.where(rest, pe, -1.0), axis=-1, keepdims=True)
    i2 = first_lane(rest & (pe == t2))
    den = t1 + t2
    gates_ref[...] = jnp.where(lane == i1, gw * (t1 / den), jnp.where(lane == i2, gw * (t2 / den), 0.0))


def _merge_router(aout, bout, x2d, w_out_b, g2, rhi, rlo):
    n = x2d.shape[0]
    tm = TM_PROJ
    row_spec = lambda w: pl.BlockSpec((tm, w), lambda i: (i, 0))
    return pl.pallas_call(
        _merge_router_kernel,
        grid=(n // tm,),
        in_specs=[
            row_spec(D_A), row_spec(D_B), row_spec(D_MODEL),
            _const_spec((D_MODEL, D_MODEL)),
            _const_spec((1, D_MODEL)),
            _const_spec((D_MODEL, ROUTER_LANES)),
            _const_spec((D_MODEL, ROUTER_LANES)),
        ],
        out_specs=(row_spec(D_MODEL), row_spec(D_MODEL), row_spec(ROUTER_LANES)),
        out_shape=(
            jax.ShapeDtypeStruct((n, D_MODEL), F32),
            jax.ShapeDtypeStruct((n, D_MODEL), BF16),
            jax.ShapeDtypeStruct((n, ROUTER_LANES), F32),
        ),
        compiler_params=_params(("parallel",)),
        name="merge_router",
    )(aout, bout, x2d, w_out_b, g2, rhi, rlo)


def _moe_kernel(h_ref, gates_ref, x1_ref, wg_ref, wu_ref, wd_ref, o_ref):
    e = pl.program_id(1)

    @pl.when(e == 0)
    def _():
        o_ref[...] = x1_ref[...]

    h = h_ref[...]
    gates = gates_ref[...]
    lane = lax.broadcasted_iota(jnp.int32, gates.shape, 1)
    ge = jnp.sum(jnp.where(lane == e, gates, 0.0), axis=-1, keepdims=True)
    a = jax.nn.silu(_dot(h, wg_ref[...])) * _dot(h, wu_ref[...]) * ge
    o_ref[...] += _dot(a.astype(BF16), wd_ref[...])


def _moe(h, gates, x1, wg, wu, wd):
    n = h.shape[0]
    tm = min(TM_MOE, n)
    return pl.pallas_call(
        _moe_kernel,
        grid=(n // tm, N_EXP),
        in_specs=[
            pl.BlockSpec((tm, D_MODEL), lambda i, e: (i, 0)),
            pl.BlockSpec((tm, ROUTER_LANES), lambda i, e: (i, 0)),
            pl.BlockSpec((tm, D_MODEL), lambda i, e: (i, 0)),
            pl.BlockSpec((None, D_MODEL, D_E), lambda i, e: (e, 0, 0)),
            pl.BlockSpec((None, D_MODEL, D_E), lambda i, e: (e, 0, 0)),
            pl.BlockSpec((None, D_E, D_MODEL), lambda i, e: (e, 0, 0)),
        ],
        out_specs=pl.BlockSpec((tm, D_MODEL), lambda i, e: (i, 0)),
        out_shape=jax.ShapeDtypeStruct((n, D_MODEL), F32),
        compiler_params=_params(("parallel", "arbitrary")),
        name="moe",
    )(h, gates, x1, wg, wu, wd)


def _router_weights(r_g, r_e):
    r = jnp.concatenate([r_e, r_g, jnp.zeros((D_MODEL, ROUTER_LANES - N_EXP - N_GROUPS), F32)], axis=1)
    hi = r.astype(BF16)
    lo = (r - hi.astype(F32)).astype(BF16)
    return hi, lo


def kernel(x_prompt, x_sample, cache_k, cache_v, page_table, norm1_g, w_in, a_vnorm_g, a_ws, a_bs, qn_g, kn_g, lam_q1, lam_k1, lam_q2, lam_k2, b_onorm_g, w_out, norm2_g, router_g, router_e, w_gate, w_up, w_down):
    batch, seq, _ = x_prompt.shape
    dec_batch, n_new, _ = x_sample.shape
    xp = x_prompt.reshape(batch * seq, D_MODEL)
    xs = x_sample.reshape(dec_batch * n_new, D_MODEL)

    ii = jnp.arange(MXU_DIM) // DK
    bd = (ii[:, None] == ii[None, :]).astype(BF16)
    eye = jnp.eye(CHUNK // n_new, dtype=F32)

    kp, vp, ks, vs, cv = [], [], [], [], []
    for l in range(DEPTH):
        lam_init = 0.8 - 0.6 * math.exp(-0.3 * l)
        w_in_b = w_in[l].astype(BF16)
        w_out_b = w_out[l].astype(BF16)
        wg, wu, wd = w_gate[l].astype(BF16), w_up[l].astype(BF16), w_down[l].astype(BF16)
        rhi, rlo = _router_weights(router_g[l], router_e[l])
        g1 = norm1_g[l][None]
        g2 = norm2_g[l][None]
        avg = a_vnorm_g[l][None]
        qg = jnp.tile(qn_g[l], D_QK // DK)[None]
        kg = jnp.tile(kn_g[l], D_QK // DK)[None]
        og = b_onorm_g[l][None]
        lams = (lam_q1[l][None], lam_k1[l][None], lam_q2[l][None], lam_k2[l][None])
        ws_p = a_ws[l]
        bias_p = jnp.repeat(a_bs[l].T, DH_A, axis=1)
        ws_s = jax.vmap(lambda w: jnp.kron(eye, w[:n_new, :n_new]))(a_ws[l])
        bias_s = jnp.tile(jnp.repeat(a_bs[l][:, :n_new].T, DH_A, axis=1), (CHUNK // n_new, 1))

        aout, q, k32, kb, v32, vb, _ = _in_proj(xp, g1, w_in_b, avg, qg, kg, ws_p, bias_p, bd)
        bout = _attn_prompt(q, kb, vb, *lams, og, lam_init, batch, seq)
        x1, h, gates = _merge_router(aout, bout, xp, w_out_b, g2, rhi, rlo)
        xp = _moe(h, gates, x1, wg, wu, wd)
        kp.append(k32.reshape(batch, seq, H_B, 2 * DK))
        vp.append(v32.reshape(batch, seq, H_B, DV))

        aout, q, k32, kb, v32, vb, va = _in_proj(xs, g1, w_in_b, avg, qg, kg, ws_s, bias_s, bd)
        bout = _attn_sample(q, kb, vb, cache_k, cache_v, page_table, l, *lams, og, lam_init, dec_batch, n_new)
        x1, h, gates = _merge_router(aout, bout, xs, w_out_b, g2, rhi, rlo)
        xs = _moe(h, gates, x1, wg, wu, wd)
        ks.append(k32.reshape(dec_batch, n_new, H_B, 2 * DK))
        vs.append(v32.reshape(dec_batch, n_new, H_B, DV))
        cv.append(va.reshape(dec_batch, n_new, D_A))

    return (xp.reshape(batch, seq, D_MODEL), xs.reshape(dec_batch, n_new, D_MODEL),
            jnp.stack(kp), jnp.stack(vp), jnp.stack(ks), jnp.stack(vs), jnp.stack(cv))
```

```python
import functools
import math

import jax
import jax.numpy as jnp
from jax import lax
from jax.experimental import pallas as pl
from jax.experimental.pallas import tpu as pltpu

F32 = jnp.float32
BF16 = jnp.bfloat16

D_MODEL = 1024
DEPTH = 2
D_A = 512
H_A = 8
DH_A = D_A // H_A
CHUNK = 128
D_B = 512
H_B = 4
DV = D_B // H_B
DK = DV // 2
D_QK = H_B * 2 * DK
D_IN = 2 * D_A + 2 * D_QK + D_B
ATTN_SCALE = DK ** -0.5
LOG2E = math.log2(math.e)
DVA = DV + 16
NEG_INF = -1e30
N_GROUPS = 4
E_PER = 4
N_EXP = N_GROUPS * E_PER
D_E = 256
EPS = 1e-6

LANES = 128
MXU_DIM = 256
VMEM_LIMIT_BYTES = 48 * 1024 * 1024

TM_PROJ = 512
TQ = 512
TK = 512
TM_MOE = 1024
ROUTER_LANES = 128
GROUP_LANE0 = N_EXP


def _params(sem):
    return pltpu.CompilerParams(dimension_semantics=sem, vmem_limit_bytes=VMEM_LIMIT_BYTES)


def _const_spec(shape):
    nd = len(shape)
    return pl.BlockSpec(shape, lambda *_: (0,) * nd)


def _rms(x):
    return x * lax.rsqrt(jnp.mean(x * x, axis=-1, keepdims=True) + EPS)


def _dot(a, b):
    return jnp.dot(a, b, preferred_element_type=F32)


def _dot_nt(a, b):
    return lax.dot_general(a, b, (((1,), (1,)), ((), ())), preferred_element_type=F32)


def _in_proj_kernel(x_ref, g1_ref, w_ref, avg_ref, qg_ref, kg_ref, ws_ref, bias_ref, bd_ref,
                    aout_ref, q_ref, k32_ref, kb_ref, v32_ref, vb_ref, va_ref, *, tm):
    x = x_ref[...]
    xb = (_rms(x) * g1_ref[...]).astype(BF16)

    a = jax.nn.gelu(_dot(xb, w_ref[:, 0:2 * D_A]))
    u = a[:, :D_A]
    va = _rms(a[:, D_A:]) * avg_ref[...]
    va_ref[...] = va
    vab = va.astype(BF16)

    row = lax.broadcasted_iota(jnp.int32, (CHUNK, CHUNK), 0)
    col = lax.broadcasted_iota(jnp.int32, (CHUNK, CHUNK), 1)
    tril = row >= col
    wsm = [jnp.where(tril, ws_ref[h], 0.0).astype(BF16) for h in range(H_A)]
    first_head = lax.broadcasted_iota(jnp.int32, (CHUNK, LANES), 1) < DH_A
    for c in range(tm // CHUNK):
        rows = slice(c * CHUNK, (c + 1) * CHUNK)
        for hp in range(H_A // 2):
            cols = slice(hp * LANES, (hp + 1) * LANES)
            blk = vab[rows, cols]
            s = jnp.where(first_head, _dot(wsm[2 * hp], blk), _dot(wsm[2 * hp + 1], blk))
            s = s + bias_ref[:, cols]
            aout_ref[rows, cols] = (u[rows, cols] * s).astype(BF16)

    bd = bd_ref[...]

    def group_norm(z, g):
        sq = (z * z).astype(BF16)
        ss = jnp.concatenate([_dot(sq[:, i * MXU_DIM:(i + 1) * MXU_DIM], bd) for i in range(D_QK // MXU_DIM)], axis=1)
        return z * lax.rsqrt(ss * (1.0 / DK) + EPS) * g

    qn = group_norm(_dot(xb, w_ref[:, 2 * D_A:2 * D_A + D_QK]), qg_ref[...])
    q_ref[...] = (qn * (ATTN_SCALE * LOG2E)).astype(BF16)
    kn = group_norm(_dot(xb, w_ref[:, 2 * D_A + D_QK:2 * D_A + 2 * D_QK]), kg_ref[...])
    k32_ref[...] = kn
    kb_ref[...] = kn.astype(BF16)
    v = _dot(xb, w_ref[:, 2 * D_A + 2 * D_QK:D_IN])
    v32_ref[...] = v
    vb_ref[...] = v.astype(BF16)


def _in_proj(x2d, g1, w_in_b, avg, qg, kg, ws, bias, bd):
    n = x2d.shape[0]
    tm = TM_PROJ
    row_spec = lambda w: pl.BlockSpec((tm, w), lambda i: (i, 0))
    out_shapes = (
        jax.ShapeDtypeStruct((n, D_A), BF16),
        jax.ShapeDtypeStruct((n, D_QK), BF16),
        jax.ShapeDtypeStruct((n, D_QK), F32),
        jax.ShapeDtypeStruct((n, D_QK), BF16),
        jax.ShapeDtypeStruct((n, D_B), F32),
        jax.ShapeDtypeStruct((n, D_B), BF16),
        jax.ShapeDtypeStruct((n, D_A), F32),
    )
    return pl.pallas_call(
        functools.partial(_in_proj_kernel, tm=tm),
        grid=(n // tm,),
        in_specs=[
            row_spec(D_MODEL),
            _const_spec((1, D_MODEL)),
            _const_spec((D_MODEL, D_IN)),
            _const_spec((1, D_A)),
            _const_spec((1, D_QK)),
            _const_spec((1, D_QK)),
            _const_spec((H_A, CHUNK, CHUNK)),
            _const_spec((CHUNK, D_A)),
            _const_spec((MXU_DIM, MXU_DIM)),
        ],
        out_specs=tuple(row_spec(s.shape[1]) for s in out_shapes),
        out_shape=out_shapes,
        compiler_params=_params(("parallel",)),
        name="in_proj",
    )(x2d, g1, w_in_b, avg, qg, kg, ws, bias, bd)


def _lam(lq1_ref, lk1_ref, lq2_ref, lk2_ref, lam_init):
    s1 = jnp.sum(lq1_ref[...] * lk1_ref[...], axis=-1, keepdims=True)
    s2 = jnp.sum(lq2_ref[...] * lk2_ref[...], axis=-1, keepdims=True)
    return jnp.exp(s1) - jnp.exp(s2) + lam_init


def _out_norm(o, og, lam_init):
    return _rms(o) * og * (1.0 - lam_init)


def _attn_prompt_kernel(q_ref, k_ref, v_ref, lq1_ref, lk1_ref, lq2_ref, lk2_ref, og_ref, o_ref,
                        vt_ref, m_ref, acc_ref, *, lam_init, seq):
    qi = pl.program_id(2)

    @pl.when(qi == 0)
    def _():
        ones = jnp.ones((DVA - DV, TK), BF16)
        for j in range(seq // TK):
            vt_ref[j, 0:DV, :] = v_ref[j * TK:(j + 1) * TK, :].astype(F32).T.astype(BF16)
            vt_ref[j, DV:DVA, :] = ones

    q = q_ref[...]
    lane = lax.broadcasted_iota(jnp.int32, (TQ, LANES), 1)
    zero = jnp.zeros_like(q)
    qs = jnp.concatenate([jnp.where(lane < DK, q, zero), jnp.where(lane >= DK, q, zero)], axis=0)

    m_ref[...] = jnp.full(m_ref.shape, NEG_INF, F32)
    acc_ref[...] = jnp.zeros(acc_ref.shape, F32)

    def step(kj, masked):
        start = pl.multiple_of(kj * TK, TK)
        st = _dot_nt(k_ref[pl.ds(start, TK), :], qs)
        if masked:
            c = lax.broadcasted_iota(jnp.int32, (TK, 2 * TQ), 1)
            qpos = qi * TQ + jnp.where(c >= TQ, c - TQ, c)
            kpos = kj * TK + lax.broadcasted_iota(jnp.int32, (TK, 2 * TQ), 0)
            st = jnp.where(kpos <= qpos, st, NEG_INF)
        m_prev = m_ref[...]
        m_new = jnp.maximum(m_prev, jnp.max(st, axis=0, keepdims=True))
        alpha = jnp.exp2(m_prev - m_new)
        p = jnp.exp2(st - m_new).astype(BF16)
        acc_ref[...] = alpha * acc_ref[...] + _dot(vt_ref[kj], p)
        m_ref[...] = m_new

    n_full = qi * (TQ // TK)

    def body(kj, carry):
        step(kj, False)
        return carry

    lax.fori_loop(0, n_full, body, 0)
    for d in range(TQ // TK):
        step(n_full + d, True)

    lam = _lam(lq1_ref, lk1_ref, lq2_ref, lk2_ref, lam_init)
    acc = acc_ref[...]
    o = acc[0:DV] / acc[DV:DV + 1]
    ot = o[:, :TQ] - lam * o[:, TQ:]
    o_ref[...] = _out_norm(ot.T, og_ref[...], lam_init).astype(BF16)


def _attn_prompt(q, kb, vb, lq1, lk1, lq2, lk2, og, lam_init, batch, seq):
    q3 = q.reshape(batch, seq, D_QK)
    k3 = kb.reshape(batch, seq, D_QK)
    v3 = vb.reshape(batch, seq, D_B)
    vec = _const_spec((1, DK))
    out = pl.pallas_call(
        functools.partial(_attn_prompt_kernel, lam_init=lam_init, seq=seq),
        grid=(batch, H_B, seq // TQ),
        in_specs=[
            pl.BlockSpec((None, TQ, LANES), lambda b, h, i: (b, i, h)),
            pl.BlockSpec((None, seq, LANES), lambda b, h, i: (b, 0, h)),
            pl.BlockSpec((None, seq, LANES), lambda b, h, i: (b, 0, h)),
            vec, vec, vec, vec,
            _const_spec((1, DV)),
        ],
        out_specs=pl.BlockSpec((None, TQ, LANES), lambda b, h, i: (b, i, h)),
        out_shape=jax.ShapeDtypeStruct((batch, seq, D_B), BF16),
        scratch_shapes=[
            pltpu.VMEM((seq // TK, DVA, TK), BF16),
            pltpu.VMEM((1, 2 * TQ), F32),
            pltpu.VMEM((DVA, 2 * TQ), F32),
        ],
        compiler_params=_params(("parallel", "parallel", "arbitrary")),
        name="attn_prompt",
    )(q3, k3, v3, lq1, lk1, lq2, lk2, og)
    return out.reshape(batch * seq, D_B)


def _attn_sample_kernel(pt_ref, q_ref, kn_ref, vn_ref, lq1_ref, lk1_ref, lq2_ref, lk2_ref, og_ref, *rest,
                        n_pages, n_new, lam_init):
    k_pages = rest[:n_pages]
    v_pages = rest[n_pages:2 * n_pages]
    o_ref, s_ref = rest[2 * n_pages:]
    n_rows = H_B * 2 * n_new
    n_cols = k_pages[0].shape[0]

    q = q_ref[...].astype(F32)
    lane = lax.broadcasted_iota(jnp.int32, (n_new, LANES), 1)
    parts = []
    for h in range(H_B):
        qh = q[:, h * LANES:(h + 1) * LANES]
        parts.append(jnp.where(lane < DK, qh, 0.0))
        parts.append(jnp.where(lane >= DK, qh, 0.0))
    qs = jnp.concatenate(parts, axis=0).astype(BF16)

    def valid_mask(cols, causal):
        r = lax.broadcasted_iota(jnp.int32, (n_rows, cols), 0)
        c = lax.broadcasted_iota(jnp.int32, (n_rows, cols), 1)
        valid = (c & (H_B - 1)) == lax.shift_right_logical(r, int(math.log2(2 * n_new)))
        if causal:
            valid = valid & (lax.shift_right_logical(c, int(math.log2(H_B))) <= (r & (n_new - 1)))
        return valid

    valid = valid_mask(n_cols, False)
    mx = jnp.full((n_rows, n_cols), NEG_INF, F32)
    for p in range(n_pages):
        s = jnp.where(valid, _dot_nt(qs, k_pages[p][...].astype(BF16)), NEG_INF)
        s_ref[:, p * n_cols:(p + 1) * n_cols] = s
        mx = jnp.maximum(mx, s)
    s_new = jnp.where(valid_mask(n_new * H_B, True), _dot_nt(qs, kn_ref[...]), NEG_INF)
    m = jnp.maximum(jnp.max(mx, axis=-1, keepdims=True), jnp.max(s_new, axis=-1, keepdims=True))

    p_new = jnp.exp2(s_new - m)
    acc = _dot(p_new.astype(BF16), vn_ref[...])
    lsum = jnp.zeros((n_rows, n_cols), F32)
    for p in range(n_pages):
        pp = jnp.exp2(s_ref[:, p * n_cols:(p + 1) * n_cols] - m)
        lsum = lsum + pp
        acc = acc + _dot(pp.astype(BF16), v_pages[p][...].astype(BF16))
    l = jnp.sum(lsum, axis=-1, keepdims=True) + jnp.sum(p_new, axis=-1, keepdims=True)

    lam = _lam(lq1_ref, lk1_ref, lq2_ref, lk2_ref, lam_init)
    o = acc / l
    for h in range(H_B):
        base = h * 2 * n_new
        oh = o[base:base + n_new] - lam * o[base + n_new:base + 2 * n_new]
        o_ref[:, h * DV:(h + 1) * DV] = _out_norm(oh, og_ref[...], lam_init)


def _attn_sample(q, kb, vb, cache_k, cache_v, page_table, layer, lq1, lk1, lq2, lk2, og, lam_init, dec_batch, n_new):
    n_pages = page_table.shape[1]
    page = cache_k.shape[2]
    n_pool = cache_k.shape[1]
    ck = cache_k.reshape(DEPTH, n_pool, page * H_B, 2 * DK)
    cv = cache_v.reshape(DEPTH, n_pool, page * H_B, DV)
    q3 = q.reshape(dec_batch, n_new, D_QK)
    kn = kb.reshape(dec_batch, n_new * H_B, 2 * DK)
    vn = vb.reshape(dec_batch, n_new * H_B, DV)
    n_rows = H_B * 2 * n_new
    vec = pl.BlockSpec((1, DK), lambda b, pt: (0, 0))

    def page_spec(p):
        return pl.BlockSpec((None, None, page * H_B, LANES), lambda b, pt: (layer, pt[b * n_pages + p], 0, 0))

    grid_spec = pltpu.PrefetchScalarGridSpec(
        num_scalar_prefetch=1,
        grid=(dec_batch,),
        in_specs=[
            pl.BlockSpec((None, n_new, D_QK), lambda b, pt: (b, 0, 0)),
            pl.BlockSpec((None, n_new * H_B, LANES), lambda b, pt: (b, 0, 0)),
            pl.BlockSpec((None, n_new * H_B, LANES), lambda b, pt: (b, 0, 0)),
            vec, vec, vec, vec,
            pl.BlockSpec((1, DV), lambda b, pt: (0, 0)),
        ] + [page_spec(p) for p in range(n_pages)] * 2,
        out_specs=pl.BlockSpec((None, n_new, D_B), lambda b, pt: (b, 0, 0)),
        scratch_shapes=[pltpu.VMEM((n_rows, n_pages * page * H_B), F32)],
    )
    out = pl.pallas_call(
        functools.partial(_attn_sample_kernel, n_pages=n_pages, n_new=n_new, lam_init=lam_init),
        grid_spec=grid_spec,
        out_shape=jax.ShapeDtypeStruct((dec_batch, n_new, D_B), F32),
        compiler_params=_params(("arbitrary",)),
        name="attn_sample",
    )(page_table.reshape(-1), q3, kn, vn, lq1, lk1, lq2, lk2, og, *([ck] * n_pages), *([cv] * n_pages))
    return out.reshape(dec_batch * n_new, D_B)


def _merge_router_kernel(a_ref, b_ref, x_ref, wo_ref, g2_ref, rhi_ref, rlo_ref, x1_ref, h_ref, gates_ref):
    y = _dot(a_ref[...].astype(BF16), wo_ref[0:D_A, :]) + _dot(b_ref[...].astype(BF16), wo_ref[D_A:, :])
    x1 = x_ref[...] + y
    x1_ref[...] = x1
    h = _rms(x1) * g2_ref[...]
    hb = h.astype(BF16)
    h_ref[...] = hb
    hlo = (h - hb.astype(F32)).astype(BF16)
    rhi = rhi_ref[...]
    logits = _dot(hb, rhi) + _dot(hlo, rhi) + _dot(hb, rlo_ref[...])

    lane = lax.broadcasted_iota(jnp.int32, logits.shape, 1)
    big = jnp.int32(ROUTER_LANES)

    def first_lane(mask):
        return jnp.min(jnp.where(mask, lane, big), axis=-1, keepdims=True)

    is_g = (lane >= GROUP_LANE0) & (lane < GROUP_LANE0 + N_GROUPS)
    lg = jnp.where(is_g, logits, NEG_INF)
    eg = jnp.where(is_g, jnp.exp(lg - jnp.max(lg, axis=-1, keepdims=True)), 0.0)
    pg = eg / jnp.sum(eg, axis=-1, keepdims=True)
    gw = jnp.max(pg, axis=-1, keepdims=True)
    gi = first_lane(is_g & (pg == gw)) - GROUP_LANE0
    sel = (lane < N_EXP) & (lax.shift_right_logical(lane, int(math.log2(E_PER))) == gi)
    le = jnp.where(sel, logits, NEG_INF)
    ee = jnp.where(sel, jnp.exp(le - jnp.max(le, axis=-1, keepdims=True)), 0.0)
    pe = ee / jnp.sum(ee, axis=-1, keepdims=True)
    t1 = jnp.max(jnp.where(sel, pe, -1.0), axis=-1, keepdims=True)
    i1 = first_lane(sel & (pe == t1))
    rest = sel & (lane != i1)
    t2 = jnp.max(jnp.where(rest, pe, -1.0), axis=-1, keepdims=True)
    i2 = first_lane(rest & (pe == t2))
    den = t1 + t2
    gates_ref[...] = jnp.where(lane == i1, gw * (t1 / den), jnp.where(lane == i2, gw * (t2 / den), 0.0))


def _merge_router(aout, bout, x2d, w_out_b, g2, rhi, rlo):
    n = x2d.shape[0]
    tm = TM_PROJ
    row_spec = lambda w: pl.BlockSpec((tm, w), lambda i: (i, 0))
    return pl.pallas_call(
        _merge_router_kernel,
        grid=(n // tm,),
        in_specs=[
            row_spec(D_A), row_spec(D_B), row_spec(D_MODEL),
            _const_spec((D_MODEL, D_MODEL)),
            _const_spec((1, D_MODEL)),
            _const_spec((D_MODEL, ROUTER_LANES)),
            _const_spec((D_MODEL, ROUTER_LANES)),
        ],
        out_specs=(row_spec(D_MODEL), row_spec(D_MODEL), row_spec(ROUTER_LANES)),
        out_shape=(
            jax.ShapeDtypeStruct((n, D_MODEL), F32),
            jax.ShapeDtypeStruct((n, D_MODEL), BF16),
            jax.ShapeDtypeStruct((n, ROUTER_LANES), F32),
        ),
        compiler_params=_params(("parallel",)),
        name="merge_router",
    )(aout, bout, x2d, w_out_b, g2, rhi, rlo)


def _moe_kernel(h_ref, gates_ref, x1_ref, wg_ref, wu_ref, wd_ref, o_ref):
    e = pl.program_id(1)

    @pl.when(e == 0)
    def _():
        o_ref[...] = x1_ref[...]

    h = h_ref[...]
    gates = gates_ref[...]
    lane = lax.broadcasted_iota(jnp.int32, gates.shape, 1)
    ge = jnp.sum(jnp.where(lane == e, gates, 0.0), axis=-1, keepdims=True)
    a = jax.nn.silu(_dot(h, wg_ref[...])) * _dot(h, wu_ref[...]) * ge
    o_ref[...] += _dot(a.astype(BF16), wd_ref[...])


def _moe(h, gates, x1, wg, wu, wd):
    n = h.shape[0]
    tm = min(TM_MOE, n)
    return pl.pallas_call(
        _moe_kernel,
        grid=(n // tm, N_EXP),
        in_specs=[
            pl.BlockSpec((tm, D_MODEL), lambda i, e: (i, 0)),
            pl.BlockSpec((tm, ROUTER_LANES), lambda i, e: (i, 0)),
            pl.BlockSpec((tm, D_MODEL), lambda i, e: (i, 0)),
            pl.BlockSpec((None, D_MODEL, D_E), lambda i, e: (e, 0, 0)),
            pl.BlockSpec((None, D_MODEL, D_E), lambda i, e: (e, 0, 0)),
            pl.BlockSpec((None, D_E, D_MODEL), lambda i, e: (e, 0, 0)),
        ],
        out_specs=pl.BlockSpec((tm, D_MODEL), lambda i, e: (i, 0)),
        out_shape=jax.ShapeDtypeStruct((n, D_MODEL), F32),
        compiler_params=_params(("parallel", "arbitrary")),
        name="moe",
    )(h, gates, x1, wg, wu, wd)


def _router_weights(r_g, r_e):
    r = jnp.concatenate([r_e, r_g, jnp.zeros((D_MODEL, ROUTER_LANES - N_EXP - N_GROUPS), F32)], axis=1)
    hi = r.astype(BF16)
    lo = (r - hi.astype(F32)).astype(BF16)
    return hi, lo


def kernel(x_prompt, x_sample, cache_k, cache_v, page_table, norm1_g, w_in, a_vnorm_g, a_ws, a_bs, qn_g, kn_g, lam_q1, lam_k1, lam_q2, lam_k2, b_onorm_g, w_out, norm2_g, router_g, router_e, w_gate, w_up, w_down):
    batch, seq, _ = x_prompt.shape
    dec_batch, n_new, _ = x_sample.shape
    xp = x_prompt.reshape(batch * seq, D_MODEL)
    xs = x_sample.reshape(dec_batch * n_new, D_MODEL)

    ii = jnp.arange(MXU_DIM) // DK
    bd = (ii[:, None] == ii[None, :]).astype(BF16)
    eye = jnp.eye(CHUNK // n_new, dtype=F32)

    kp, vp, ks, vs, cv = [], [], [], [], []
    for l in range(DEPTH):
        lam_init = 0.8 - 0.6 * math.exp(-0.3 * l)
        w_in_b = w_in[l].astype(BF16)
        w_out_b = w_out[l].astype(BF16)
        wg, wu, wd = w_gate[l].astype(BF16), w_up[l].astype(BF16), w_down[l].astype(BF16)
        rhi, rlo = _router_weights(router_g[l], router_e[l])
        g1 = norm1_g[l][None]
        g2 = norm2_g[l][None]
        avg = a_vnorm_g[l][None]
        qg = jnp.tile(qn_g[l], D_QK // DK)[None]
        kg = jnp.tile(kn_g[l], D_QK // DK)[None]
        og = b_onorm_g[l][None]
        lams = (lam_q1[l][None], lam_k1[l][None], lam_q2[l][None], lam_k2[l][None])
        ws_p = a_ws[l]
        bias_p = jnp.repeat(a_bs[l].T, DH_A, axis=1)
        ws_s = jax.vmap(lambda w: jnp.kron(eye, w[:n_new, :n_new]))(a_ws[l])
        bias_s = jnp.tile(jnp.repeat(a_bs[l][:, :n_new].T, DH_A, axis=1), (CHUNK // n_new, 1))

        aout, q, k32, kb, v32, vb, _ = _in_proj(xp, g1, w_in_b, avg, qg, kg, ws_p, bias_p, bd)
        bout = _attn_prompt(q, kb, vb, *lams, og, lam_init, batch, seq)
        x1, h, gates = _merge_router(aout, bout, xp, w_out_b, g2, rhi, rlo)
        xp = _moe(h, gates, x1, wg, wu, wd)
        kp.append(k32.reshape(batch, seq, H_B, 2 * DK))
        vp.append(v32.reshape(batch, seq, H_B, DV))

        aout, q, k32, kb, v32, vb, va = _in_proj(xs, g1, w_in_b, avg, qg, kg, ws_s, bias_s, bd)
        bout = _attn_sample(q, kb, vb, cache_k, cache_v, page_table, l, *lams, og, lam_init, dec_batch, n_new)
        x1, h, gates = _merge_router(aout, bout, xs, w_out_b, g2, rhi, rlo)
        xs = _moe(h, gates, x1, wg, wu, wd)
        ks.append(k32.reshape(dec_batch, n_new, H_B, 2 * DK))
        vs.append(v32.reshape(dec_batch, n_new, H_B, DV))
        cv.append(va.reshape(dec_batch, n_new, D_A))

    return (xp.reshape(batch, seq, D_MODEL), xs.reshape(dec_batch, n_new, D_MODEL),
            jnp.stack(kp), jnp.stack(vp), jnp.stack(ks), jnp.stack(vs), jnp.stack(cv))
```

```python
import functools
import math

import jax
import jax.numpy as jnp
from jax import lax
from jax.experimental import pallas as pl
from jax.experimental.pallas import tpu as pltpu

F32 = jnp.float32
BF16 = jnp.bfloat16

D_MODEL = 1024
DEPTH = 2
D_A = 512
H_A = 8
DH_A = D_A // H_A
CHUNK = 128
D_B = 512
H_B = 4
DV = D_B // H_B
DK = DV // 2
D_QK = H_B * 2 * DK
D_IN = 2 * D_A + 2 * D_QK + D_B
ATTN_SCALE = DK ** -0.5
LOG2E = math.log2(math.e)
DVA = DV + 16
NEG_INF = -1e30
N_GROUPS = 4
E_PER = 4
N_EXP = N_GROUPS * E_PER
D_E = 256
EPS = 1e-6

LANES = 128
MXU_DIM = 256
VMEM_LIMIT_BYTES = 48 * 1024 * 1024

TM_PROJ = 512
TQ = 512
TK = 512
assert TQ == TK
TM_MOE = 1024
ROUTER_LANES = 128
GROUP_LANE0 = N_EXP


def _params(sem):
    return pltpu.CompilerParams(dimension_semantics=sem, vmem_limit_bytes=VMEM_LIMIT_BYTES)


def _const_spec(shape):
    nd = len(shape)
    return pl.BlockSpec(shape, lambda *_: (0,) * nd)


def _rms(x):
    return x * lax.rsqrt(jnp.mean(x * x, axis=-1, keepdims=True) + EPS)


def _dot(a, b):
    return jnp.dot(a, b, preferred_element_type=F32)


def _dot_nt(a, b):
    return lax.dot_general(a, b, (((1,), (1,)), ((), ())), preferred_element_type=F32)


N_PROJ_IN = 9


def _in_proj_kernel(*refs, tm, n_alias, emit_va):
    x_ref, g1_ref, w_ref, avg_ref, qg_ref, kg_ref, ws_ref, bias_ref, bd_ref = refs[:N_PROJ_IN]
    outs = refs[N_PROJ_IN + n_alias:]
    aout_ref, q_ref, k32_ref, kb_ref, v32_ref, vb_ref = outs[:6]
    x = x_ref[...]
    xb = (_rms(x) * g1_ref[...]).astype(BF16)

    a = jax.nn.gelu(_dot(xb, w_ref[:, 0:2 * D_A]))
    u = a[:, :D_A]
    va = _rms(a[:, D_A:]) * avg_ref[...]
    if emit_va:
        outs[6][...] = va
    vab = va.astype(BF16)

    row = lax.broadcasted_iota(jnp.int32, (CHUNK, CHUNK), 0)
    col = lax.broadcasted_iota(jnp.int32, (CHUNK, CHUNK), 1)
    tril = row >= col
    wsm = [jnp.where(tril, ws_ref[h], 0.0).astype(BF16) for h in range(H_A)]
    first_head = lax.broadcasted_iota(jnp.int32, (CHUNK, LANES), 1) < DH_A
    for c in range(tm // CHUNK):
        rows = slice(c * CHUNK, (c + 1) * CHUNK)
        for hp in range(H_A // 2):
            cols = slice(hp * LANES, (hp + 1) * LANES)
            blk = vab[rows, cols]
            s = jnp.where(first_head, _dot(wsm[2 * hp], blk), _dot(wsm[2 * hp + 1], blk))
            s = s + bias_ref[:, cols]
            aout_ref[rows, cols] = (u[rows, cols] * s).astype(BF16)

    bd = bd_ref[...]

    def group_norm(z, g):
        sq = (z * z).astype(BF16)
        ss = jnp.concatenate([_dot(sq[:, i * MXU_DIM:(i + 1) * MXU_DIM], bd) for i in range(D_QK // MXU_DIM)], axis=1)
        return z * lax.rsqrt(ss * (1.0 / DK) + EPS) * g

    qn = group_norm(_dot(xb, w_ref[:, 2 * D_A:2 * D_A + D_QK]), qg_ref[...])
    q_ref[...] = (qn * (ATTN_SCALE * LOG2E)).astype(BF16)
    kn = group_norm(_dot(xb, w_ref[:, 2 * D_A + D_QK:2 * D_A + 2 * D_QK]), kg_ref[...])
    kb_ref[...] = kn.astype(BF16)
    v = _dot(xb, w_ref[:, 2 * D_A + 2 * D_QK:D_IN])
    vb_ref[...] = v.astype(BF16)
    for h in range(H_B):
        k32_ref[pl.ds(h, tm, stride=H_B), :] = kn[:, h * LANES:(h + 1) * LANES]
        v32_ref[pl.ds(h, tm, stride=H_B), :] = v[:, h * LANES:(h + 1) * LANES]


def _in_proj(x2d, g1, w_in_b, avg, qg, kg, ws, bias, bd, layer, prev, emit_va):
    n = x2d.shape[0]
    tm = TM_PROJ
    row_spec = lambda w: pl.BlockSpec((tm, w), lambda i: (i, 0))
    rows4 = jax.ShapeDtypeStruct((DEPTH, n * H_B, LANES), F32)
    rows4_spec = pl.BlockSpec((None, tm * H_B, LANES), lambda i: (layer, i, 0))
    out_shapes = [
        jax.ShapeDtypeStruct((n, D_A), BF16),
        jax.ShapeDtypeStruct((n, D_QK), BF16),
        rows4,
        jax.ShapeDtypeStruct((n, D_QK), BF16),
        rows4,
        jax.ShapeDtypeStruct((n, D_B), BF16),
    ]
    out_specs = [row_spec(D_A), row_spec(D_QK), rows4_spec, row_spec(D_QK), rows4_spec, row_spec(D_B)]
    if emit_va:
        out_shapes.append(jax.ShapeDtypeStruct((DEPTH, n, D_A), F32))
        out_specs.append(pl.BlockSpec((None, tm, D_A), lambda i: (layer, i, 0)))
    prev = () if prev is None else tuple(prev)
    alias_out = (2, 4, 6)[:len(prev)]
    return pl.pallas_call(
        functools.partial(_in_proj_kernel, tm=tm, n_alias=len(prev), emit_va=emit_va),
        grid=(n // tm,),
        in_specs=[
            row_spec(D_MODEL),
            _const_spec((1, D_MODEL)),
            pl.BlockSpec((None, D_MODEL, D_IN), lambda i: (layer, 0, 0)),
            _const_spec((1, D_A)),
            _const_spec((1, D_QK)),
            _const_spec((1, D_QK)),
            _const_spec((H_A, CHUNK, CHUNK)),
            _const_spec((CHUNK, D_A)),
            _const_spec((MXU_DIM, MXU_DIM)),
        ] + [pl.BlockSpec(memory_space=pl.ANY)] * len(prev),
        out_specs=tuple(out_specs),
        out_shape=tuple(out_shapes),
        input_output_aliases={N_PROJ_IN + j: o for j, o in enumerate(alias_out)},
        compiler_params=_params(("parallel",)),
        name="in_proj",
    )(x2d, g1, w_in_b, avg, qg, kg, ws, bias, bd, *prev)


def _lam(lq1_ref, lk1_ref, lq2_ref, lk2_ref, lam_init):
    s1 = jnp.sum(lq1_ref[...] * lk1_ref[...], axis=-1, keepdims=True)
    s2 = jnp.sum(lq2_ref[...] * lk2_ref[...], axis=-1, keepdims=True)
    return jnp.exp(s1) - jnp.exp(s2) + lam_init


def _out_norm(o, og, lam_init):
    return _rms(o) * og * (1.0 - lam_init)


def _attn_prompt_kernel(q_ref, k_ref, v_ref, lq1_ref, lk1_ref, lq2_ref, lk2_ref, og_ref, o_ref,
                        vt_ref, sa_ref, sb_ref, m_ref, acc_ref, *, lam_init, seq):
    qi = pl.program_id(2)

    @pl.when(qi == 0)
    def _():
        ones = jnp.ones((DVA - DV, TK), BF16)
        for j in range(seq // TK):
            vt_ref[j, 0:DV, :] = v_ref[j * TK:(j + 1) * TK, :].astype(F32).T.astype(BF16)
            vt_ref[j, DV:DVA, :] = ones

    q = q_ref[...]
    lane = lax.broadcasted_iota(jnp.int32, (TQ, LANES), 1)
    zero = jnp.zeros_like(q)
    qs = jnp.concatenate([jnp.where(lane < DK, q, zero), jnp.where(lane >= DK, q, zero)], axis=0)

    m_ref[...] = jnp.full(m_ref.shape, NEG_INF, F32)
    acc_ref[...] = jnp.zeros(acc_ref.shape, F32)

    def scores(kj, st_ref):
        start = pl.multiple_of(kj * TK, TK)
        st_ref[...] = _dot_nt(k_ref[pl.ds(start, TK), :], qs)

    def step(kj, st_ref, masked):
        def read():
            st = st_ref[...]
            if masked:
                c = lax.broadcasted_iota(jnp.int32, (TK, 2 * TQ), 1)
                qpos = qi * TQ + jnp.where(c >= TQ, c - TQ, c)
                kpos = kj * TK + lax.broadcasted_iota(jnp.int32, (TK, 2 * TQ), 0)
                st = jnp.where(kpos <= qpos, st, NEG_INF)
            return st

        m_prev = m_ref[...]
        m_new = jnp.maximum(m_prev, jnp.max(read(), axis=0, keepdims=True))
        alpha = jnp.exp2(m_prev - m_new)
        p = jnp.exp2(read() - m_new).astype(BF16)
        acc_ref[...] = alpha * acc_ref[...] + _dot(vt_ref[kj], p)
        m_ref[...] = m_new

    scores(0, sa_ref)

    def pair(i, carry):
        kj = 2 * i
        scores(kj + 1, sb_ref)
        step(kj, sa_ref, False)
        scores(kj + 2, sa_ref)
        step(kj + 1, sb_ref, False)
        return carry

    lax.fori_loop(0, lax.shift_right_logical(qi, 1), pair, 0)

    @pl.when((qi & 1) == 1)
    def _():
        scores(qi, sb_ref)
        step(qi - 1, sa_ref, False)
        step(qi, sb_ref, True)

    @pl.when((qi & 1) == 0)
    def _():
        step(qi, sa_ref, True)

    lam = _lam(lq1_ref, lk1_ref, lq2_ref, lk2_ref, lam_init)
    acc = acc_ref[...]
    o = acc[0:DV] / acc[DV:DV + 1]
    ot = o[:, :TQ] - lam * o[:, TQ:]
    o_ref[...] = _out_norm(ot.T, og_ref[...], lam_init).astype(BF16)


def _attn_prompt(q, kb, vb, lq1, lk1, lq2, lk2, og, lam_init, batch, seq):
    q3 = q.reshape(batch, seq, D_QK)
    k3 = kb.reshape(batch, seq, D_QK)
    v3 = vb.reshape(batch, seq, D_B)
    vec = _const_spec((1, DK))
    out = pl.pallas_call(
        functools.partial(_attn_prompt_kernel, lam_init=lam_init, seq=seq),
        grid=(batch, H_B, seq // TQ),
        in_specs=[
            pl.BlockSpec((None, TQ, LANES), lambda b, h, i: (b, i, h)),
            pl.BlockSpec((None, seq, LANES), lambda b, h, i: (b, 0, h)),
            pl.BlockSpec((None, seq, LANES), lambda b, h, i: (b, 0, h)),
            vec, vec, vec, vec,
            _const_spec((1, DV)),
        ],
        out_specs=pl.BlockSpec((None, TQ, LANES), lambda b, h, i: (b, i, h)),
        out_shape=jax.ShapeDtypeStruct((batch, seq, D_B), BF16),
        scratch_shapes=[
            pltpu.VMEM((seq // TK, DVA, TK), BF16),
            pltpu.VMEM((TK, 2 * TQ), F32),
            pltpu.VMEM((TK, 2 * TQ), F32),
            pltpu.VMEM((1, 2 * TQ), F32),
            pltpu.VMEM((DVA, 2 * TQ), F32),
        ],
        compiler_params=_params(("parallel", "parallel", "arbitrary")),
        name="attn_prompt",
    )(q3, k3, v3, lq1, lk1, lq2, lk2, og)
    return out.reshape(batch * seq, D_B)


def _attn_sample_kernel(pt_ref, q_ref, kn_ref, vn_ref, lq1_ref, lk1_ref, lq2_ref, lk2_ref, og_ref, *rest,
                        n_pages, n_new, lam_init):
    k_pages = rest[:n_pages]
    v_pages = rest[n_pages:2 * n_pages]
    o_ref, s_ref = rest[2 * n_pages:]
    n_rows = H_B * 2 * n_new
    n_cols = k_pages[0].shape[0]

    q = q_ref[...].astype(F32)
    lane = lax.broadcasted_iota(jnp.int32, (n_new, LANES), 1)
    parts = []
    for h in range(H_B):
        qh = q[:, h * LANES:(h + 1) * LANES]
        parts.append(jnp.where(lane < DK, qh, 0.0))
        parts.append(jnp.where(lane >= DK, qh, 0.0))
    qs = jnp.concatenate(parts, axis=0).astype(BF16)

    def valid_mask(cols, causal):
        r = lax.broadcasted_iota(jnp.int32, (n_rows, cols), 0)
        c = lax.broadcasted_iota(jnp.int32, (n_rows, cols), 1)
        valid = (c & (H_B - 1)) == lax.shift_right_logical(r, int(math.log2(2 * n_new)))
        if causal:
            valid = valid & (lax.shift_right_logical(c, int(math.log2(H_B))) <= (r & (n_new - 1)))
        return valid

    valid = valid_mask(n_cols, False)
    mx = jnp.full((n_rows, n_cols), NEG_INF, F32)
    for p in range(n_pages):
        s = jnp.where(valid, _dot_nt(qs, k_pages[p][...].astype(BF16)), NEG_INF)
        s_ref[:, p * n_cols:(p + 1) * n_cols] = s
        mx = jnp.maximum(mx, s)
    s_new = jnp.where(valid_mask(n_new * H_B, True), _dot_nt(qs, kn_ref[...]), NEG_INF)
    m = jnp.maximum(jnp.max(mx, axis=-1, keepdims=True), jnp.max(s_new, axis=-1, keepdims=True))

    p_new = jnp.exp2(s_new - m)
    acc = _dot(p_new.astype(BF16), vn_ref[...])
    lsum = jnp.zeros((n_rows, n_cols), F32)
    for p in range(n_pages):
        pp = jnp.exp2(s_ref[:, p * n_cols:(p + 1) * n_cols] - m)
        lsum = lsum + pp
        acc = acc + _dot(pp.astype(BF16), v_pages[p][...].astype(BF16))
    l = jnp.sum(lsum, axis=-1, keepdims=True) + jnp.sum(p_new, axis=-1, keepdims=True)

    lam = _lam(lq1_ref, lk1_ref, lq2_ref, lk2_ref, lam_init)
    o = acc / l
    for h in range(H_B):
        base = h * 2 * n_new
        oh = o[base:base + n_new] - lam * o[base + n_new:base + 2 * n_new]
        o_ref[:, h * DV:(h + 1) * DV] = _out_norm(oh, og_ref[...], lam_init)


def _attn_sample(q, kb, vb, cache_k, cache_v, page_table, layer, lq1, lk1, lq2, lk2, og, lam_init, dec_batch, n_new):
    n_pages = page_table.shape[1]
    page = cache_k.shape[2]
    n_pool = cache_k.shape[1]
    ck = cache_k.reshape(DEPTH, n_pool, page * H_B, 2 * DK)
    cv = cache_v.reshape(DEPTH, n_pool, page * H_B, DV)
    q3 = q.reshape(dec_batch, n_new, D_QK)
    kn = kb.reshape(dec_batch, n_new * H_B, 2 * DK)
    vn = vb.reshape(dec_batch, n_new * H_B, DV)
    n_rows = H_B * 2 * n_new
    vec = pl.BlockSpec((1, DK), lambda b, pt: (0, 0))

    def page_spec(p):
        return pl.BlockSpec((None, None, page * H_B, LANES), lambda b, pt: (layer, pt[b * n_pages + p], 0, 0))

    grid_spec = pltpu.PrefetchScalarGridSpec(
        num_scalar_prefetch=1,
        grid=(dec_batch,),
        in_specs=[
            pl.BlockSpec((None, n_new, D_QK), lambda b, pt: (b, 0, 0)),
            pl.BlockSpec((None, n_new * H_B, LANES), lambda b, pt: (b, 0, 0)),
            pl.BlockSpec((None, n_new * H_B, LANES), lambda b, pt: (b, 0, 0)),
            vec, vec, vec, vec,
            pl.BlockSpec((1, DV), lambda b, pt: (0, 0)),
        ] + [page_spec(p) for p in range(n_pages)] * 2,
        out_specs=pl.BlockSpec((None, n_new, D_B), lambda b, pt: (b, 0, 0)),
        scratch_shapes=[pltpu.VMEM((n_rows, n_pages * page * H_B), F32)],
    )
    out = pl.pallas_call(
        functools.partial(_attn_sample_kernel, n_pages=n_pages, n_new=n_new, lam_init=lam_init),
        grid_spec=grid_spec,
        out_shape=jax.ShapeDtypeStruct((dec_batch, n_new, D_B), F32),
        compiler_params=_params(("arbitrary",)),
        name="attn_sample",
    )(page_table.reshape(-1), q3, kn, vn, lq1, lk1, lq2, lk2, og, *([ck] * n_pages), *([cv] * n_pages))
    return out.reshape(dec_batch * n_new, D_B)


def _merge_router_kernel(a_ref, b_ref, x_ref, wo_ref, g2_ref, rhi_ref, rlo_ref, x1_ref, h_ref, gates_ref):
    y = _dot(a_ref[...].astype(BF16), wo_ref[0:D_A, :]) + _dot(b_ref[...].astype(BF16), wo_ref[D_A:, :])
    x1 = x_ref[...] + y
    x1_ref[...] = x1
    h = _rms(x1) * g2_ref[...]
    hb = h.astype(BF16)
    h_ref[...] = hb
    hlo = (h - hb.astype(F32)).astype(BF16)
    rhi = rhi_ref[...]
    logits = _dot(hb, rhi) + _dot(hlo, rhi) + _dot(hb, rlo_ref[...])

    lane = lax.broadcasted_iota(jnp.int32, logits.shape, 1)
    big = jnp.int32(ROUTER_LANES)

    def first_lane(mask):
        return jnp.min(jnp.where(mask, lane, big), axis=-1, keepdims=True)

    is_g = (lane >= GROUP_LANE0) & (lane < GROUP_LANE0 + N_GROUPS)
    lg = jnp.where(is_g, logits, NEG_INF)
    eg = jnp.where(is_g, jnp.exp(lg - jnp.max(lg, axis=-1, keepdims=True)), 0.0)
    pg = eg / jnp.sum(eg, axis=-1, keepdims=True)
    gw = jnp.max(pg, axis=-1, keepdims=True)
    gi = first_lane(is_g & (pg == gw)) - GROUP_LANE0
    sel = (lane < N_EXP) & (lax.shift_right_logical(lane, int(math.log2(E_PER))) == gi)
    le = jnp.where(sel, logits, NEG_INF)
    ee = jnp.where(sel, jnp.exp(le - jnp.max(le, axis=-1, keepdims=True)), 0.0)
    pe = ee / jnp.sum(ee, axis=-1, keepdims=True)
    t1 = jnp.max(jnp.where(sel, pe, -1.0), axis=-1, keepdims=True)
    i1 = first_lane(sel & (pe == t1))
    rest = sel & (lane != i1)
    t2 = jnp.max(jnp.where(rest, pe, -1.0), axis=-1, keepdims=True)
    i2 = first_lane(rest & (pe == t2))
    den = t1 + t2
    gates_ref[...] = jnp.where(lane == i1, gw * (t1 / den), jnp.where(lane == i2, gw * (t2 / den), 0.0))


def _merge_router(aout, bout, x2d, w_out_b, layer, g2, rhi, rlo):
    n = x2d.shape[0]
    tm = TM_PROJ
    row_spec = lambda w: pl.BlockSpec((tm, w), lambda i: (i, 0))
    return pl.pallas_call(
        _merge_router_kernel,
        grid=(n // tm,),
        in_specs=[
            row_spec(D_A), row_spec(D_B), row_spec(D_MODEL),
            pl.BlockSpec((None, D_MODEL, D_MODEL), lambda i: (layer, 0, 0)),
            _const_spec((1, D_MODEL)),
            _const_spec((D_MODEL, ROUTER_LANES)),
            _const_spec((D_MODEL, ROUTER_LANES)),
        ],
        out_specs=(row_spec(D_MODEL), row_spec(D_MODEL), row_spec(ROUTER_LANES)),
        out_shape=(
            jax.ShapeDtypeStruct((n, D_MODEL), F32),
            jax.ShapeDtypeStruct((n, D_MODEL), BF16),
            jax.ShapeDtypeStruct((n, ROUTER_LANES), F32),
        ),
        compiler_params=_params(("parallel",)),
        name="merge_router",
    )(aout, bout, x2d, w_out_b, g2, rhi, rlo)


def _moe_kernel(h_ref, gates_ref, x1_ref, wg_ref, wu_ref, wd_ref, o_ref):
    e = pl.program_id(1)

    @pl.when(e == 0)
    def _():
        o_ref[...] = x1_ref[...]

    h = h_ref[...]
    gates = gates_ref[...]
    lane = lax.broadcasted_iota(jnp.int32, gates.shape, 1)
    ge = jnp.sum(jnp.where(lane == e, gates, 0.0), axis=-1, keepdims=True)
    a = jax.nn.silu(_dot(h, wg_ref[...])) * _dot(h, wu_ref[...]) * ge
    o_ref[...] += _dot(a.astype(BF16), wd_ref[...])


def _moe(h, gates, x1, wg, wu, wd, layer):
    n = h.shape[0]
    tm = min(TM_MOE, n)
    return pl.pallas_call(
        _moe_kernel,
        grid=(n // tm, N_EXP),
        in_specs=[
            pl.BlockSpec((tm, D_MODEL), lambda i, e: (i, 0)),
            pl.BlockSpec((tm, ROUTER_LANES), lambda i, e: (i, 0)),
            pl.BlockSpec((tm, D_MODEL), lambda i, e: (i, 0)),
            pl.BlockSpec((None, None, D_MODEL, D_E), lambda i, e: (layer, e, 0, 0)),
            pl.BlockSpec((None, None, D_MODEL, D_E), lambda i, e: (layer, e, 0, 0)),
            pl.BlockSpec((None, None, D_E, D_MODEL), lambda i, e: (layer, e, 0, 0)),
        ],
        out_specs=pl.BlockSpec((tm, D_MODEL), lambda i, e: (i, 0)),
        out_shape=jax.ShapeDtypeStruct((n, D_MODEL), F32),
        compiler_params=_params(("parallel", "arbitrary")),
        name="moe",
    )(h, gates, x1, wg, wu, wd)


def _router_weights(r_g, r_e):
    r = jnp.concatenate([r_e, r_g, jnp.zeros((D_MODEL, ROUTER_LANES - N_EXP - N_GROUPS), F32)], axis=1)
    hi = r.astype(BF16)
    lo = (r - hi.astype(F32)).astype(BF16)
    return hi, lo


def kernel(x_prompt, x_sample, cache_k, cache_v, page_table, norm1_g, w_in, a_vnorm_g, a_ws, a_bs, qn_g, kn_g, lam_q1, lam_k1, lam_q2, lam_k2, b_onorm_g, w_out, norm2_g, router_g, router_e, w_gate, w_up, w_down):
    batch, seq, _ = x_prompt.shape
    dec_batch, n_new, _ = x_sample.shape
    xp = x_prompt.reshape(batch * seq, D_MODEL)
    xs = x_sample.reshape(dec_batch * n_new, D_MODEL)

    ii = jnp.arange(MXU_DIM) // DK
    bd = (ii[:, None] == ii[None, :]).astype(BF16)
    eye = jnp.eye(CHUNK // n_new, dtype=F32)

    w_in_b, w_out_b = w_in.astype(BF16), w_out.astype(BF16)
    wg, wu, wd = w_gate.astype(BF16), w_up.astype(BF16), w_down.astype(BF16)

    rows_p, rows_s = None, None
    for l in range(DEPTH):
        lam_init = 0.8 - 0.6 * math.exp(-0.3 * l)
        rhi, rlo = _router_weights(router_g[l], router_e[l])
        g1 = norm1_g[l][None]
        g2 = norm2_g[l][None]
        avg = a_vnorm_g[l][None]
        qg = jnp.tile(qn_g[l], D_QK // DK)[None]
        kg = jnp.tile(kn_g[l], D_QK // DK)[None]
        og = b_onorm_g[l][None]
        lams = (lam_q1[l][None], lam_k1[l][None], lam_q2[l][None], lam_k2[l][None])
        ws_p = a_ws[l]
        bias_p = jnp.repeat(a_bs[l].T, DH_A, axis=1)
        ws_s = jax.vmap(lambda w: jnp.kron(eye, w[:n_new, :n_new]))(a_ws[l])
        bias_s = jnp.tile(jnp.repeat(a_bs[l][:, :n_new].T, DH_A, axis=1), (CHUNK // n_new, 1))

        aout, q, k_all, kb, v_all, vb = _in_proj(xp, g1, w_in_b, avg, qg, kg, ws_p, bias_p, bd, l, rows_p, False)
        rows_p = (k_all, v_all)
        bout = _attn_prompt(q, kb, vb, *lams, og, lam_init, batch, seq)
        x1, h, gates = _merge_router(aout, bout, xp, w_out_b, l, g2, rhi, rlo)
        xp = _moe(h, gates, x1, wg, wu, wd, l)

        aout, q, k_all, kb, v_all, vb, va_all = _in_proj(xs, g1, w_in_b, avg, qg, kg, ws_s, bias_s, bd, l, rows_s, True)
        rows_s = (k_all, v_all, va_all)
        bout = _attn_sample(q, kb, vb, cache_k, cache_v, page_table, l, *lams, og, lam_init, dec_batch, n_new)
        x1, h, gates = _merge_router(aout, bout, xs, w_out_b, l, g2, rhi, rlo)
        xs = _moe(h, gates, x1, wg, wu, wd, l)

    return (xp.reshape(batch, seq, D_MODEL), xs.reshape(dec_batch, n_new, D_MODEL),
            rows_p[0].reshape(DEPTH, batch, seq, H_B, 2 * DK), rows_p[1].reshape(DEPTH, batch, seq, H_B, DV),
            rows_s[0].reshape(DEPTH, dec_batch, n_new, H_B, 2 * DK), rows_s[1].reshape(DEPTH, dec_batch, n_new, H_B, DV),
            rows_s[2].reshape(DEPTH, dec_batch, n_new, D_A))
```

```python
import functools
import math

import jax
import jax.numpy as jnp
from jax import lax
from jax.experimental import pallas as pl
from jax.experimental.pallas import tpu as pltpu

F32 = jnp.float32
BF16 = jnp.bfloat16

D_MODEL = 1024
DEPTH = 2
D_A = 512
H_A = 8
DH_A = D_A // H_A
CHUNK = 128
D_B = 512
H_B = 4
DV = D_B // H_B
DK = DV // 2
D_QK = H_B * 2 * DK
D_IN = 2 * D_A + 2 * D_QK + D_B
ATTN_SCALE = DK ** -0.5
LOG2E = math.log2(math.e)
DVA = DV + 16
NEG_INF = -1e30
N_GROUPS = 4
E_PER = 4
N_EXP = N_GROUPS * E_PER
D_E = 256
EPS = 1e-6

LANES = 128
MXU_DIM = 256
VMEM_LIMIT_BYTES = 48 * 1024 * 1024

TM_PROJ = 512
TQ = 512
TK = 512
assert TQ == TK
TM_MOE = 1024
ROUTER_LANES = 128
GROUP_LANE0 = N_EXP
XG_W = D_MODEL + ROUTER_LANES
META_GROUP_LANE = N_EXP
META_RANK_LANE = N_EXP + 1
TM_SORT = 512


def _params(sem):
    return pltpu.CompilerParams(dimension_semantics=sem, vmem_limit_bytes=VMEM_LIMIT_BYTES)


def _const_spec(shape):
    nd = len(shape)
    return pl.BlockSpec(shape, lambda *_: (0,) * nd)


def _rms(x):
    return x * lax.rsqrt(jnp.mean(x * x, axis=-1, keepdims=True) + EPS)


def _dot(a, b):
    return jnp.dot(a, b, preferred_element_type=F32)


def _dot_nt(a, b):
    return lax.dot_general(a, b, (((1,), (1,)), ((), ())), preferred_element_type=F32)


N_PROJ_IN = 9


def _in_proj_kernel(*refs, tm, n_alias, emit_va):
    x_ref, g1_ref, w_ref, avg_ref, qg_ref, kg_ref, ws_ref, bias_ref, bd_ref = refs[:N_PROJ_IN]
    outs = refs[N_PROJ_IN + n_alias:]
    aout_ref, q_ref, k32_ref, kb_ref, v32_ref, vb_ref = outs[:6]
    x = x_ref[...]
    xb = (_rms(x) * g1_ref[...]).astype(BF16)

    a = jax.nn.gelu(_dot(xb, w_ref[:, 0:2 * D_A]))
    u = a[:, :D_A]
    va = _rms(a[:, D_A:]) * avg_ref[...]
    if emit_va:
        outs[6][...] = va
    vab = va.astype(BF16)

    row = lax.broadcasted_iota(jnp.int32, (CHUNK, CHUNK), 0)
    col = lax.broadcasted_iota(jnp.int32, (CHUNK, CHUNK), 1)
    tril = row >= col
    wsm = [jnp.where(tril, ws_ref[h], 0.0).astype(BF16) for h in range(H_A)]
    first_head = lax.broadcasted_iota(jnp.int32, (CHUNK, LANES), 1) < DH_A
    for c in range(tm // CHUNK):
        rows = slice(c * CHUNK, (c + 1) * CHUNK)
        for hp in range(H_A // 2):
            cols = slice(hp * LANES, (hp + 1) * LANES)
            blk = vab[rows, cols]
            s = jnp.where(first_head, _dot(wsm[2 * hp], blk), _dot(wsm[2 * hp + 1], blk))
            s = s + bias_ref[:, cols]
            aout_ref[rows, cols] = (u[rows, cols] * s).astype(BF16)

    bd = bd_ref[...]

    def group_norm(z, g):
        sq = (z * z).astype(BF16)
        ss = jnp.concatenate([_dot(sq[:, i * MXU_DIM:(i + 1) * MXU_DIM], bd) for i in range(D_QK // MXU_DIM)], axis=1)
        return z * lax.rsqrt(ss * (1.0 / DK) + EPS) * g

    qn = group_norm(_dot(xb, w_ref[:, 2 * D_A:2 * D_A + D_QK]), qg_ref[...])
    q_ref[...] = (qn * (ATTN_SCALE * LOG2E)).astype(BF16)
    kn = group_norm(_dot(xb, w_ref[:, 2 * D_A + D_QK:2 * D_A + 2 * D_QK]), kg_ref[...])
    kb_ref[...] = kn.astype(BF16)
    v = _dot(xb, w_ref[:, 2 * D_A + 2 * D_QK:D_IN])
    vb_ref[...] = v.astype(BF16)
    for h in range(H_B):
        k32_ref[pl.ds(h, tm, stride=H_B), :] = kn[:, h * LANES:(h + 1) * LANES]
        v32_ref[pl.ds(h, tm, stride=H_B), :] = v[:, h * LANES:(h + 1) * LANES]


def _in_proj(x2d, g1, w_in_b, avg, qg, kg, ws, bias, bd, layer, prev, emit_va):
    n = x2d.shape[0]
    tm = TM_PROJ
    row_spec = lambda w: pl.BlockSpec((tm, w), lambda i: (i, 0))
    rows4 = jax.ShapeDtypeStruct((DEPTH, n * H_B, LANES), F32)
    rows4_spec = pl.BlockSpec((None, tm * H_B, LANES), lambda i: (layer, i, 0))
    out_shapes = [
        jax.ShapeDtypeStruct((n, D_A), BF16),
        jax.ShapeDtypeStruct((n, D_QK), BF16),
        rows4,
        jax.ShapeDtypeStruct((n, D_QK), BF16),
        rows4,
        jax.ShapeDtypeStruct((n, D_B), BF16),
    ]
    out_specs = [row_spec(D_A), row_spec(D_QK), rows4_spec, row_spec(D_QK), rows4_spec, row_spec(D_B)]
    if emit_va:
        out_shapes.append(jax.ShapeDtypeStruct((DEPTH, n, D_A), F32))
        out_specs.append(pl.BlockSpec((None, tm, D_A), lambda i: (layer, i, 0)))
    prev = () if prev is None else tuple(prev)
    alias_out = (2, 4, 6)[:len(prev)]
    return pl.pallas_call(
        functools.partial(_in_proj_kernel, tm=tm, n_alias=len(prev), emit_va=emit_va),
        grid=(n // tm,),
        in_specs=[
            row_spec(D_MODEL),
            _const_spec((1, D_MODEL)),
            pl.BlockSpec((None, D_MODEL, D_IN), lambda i: (layer, 0, 0)),
            _const_spec((1, D_A)),
            _const_spec((1, D_QK)),
            _const_spec((1, D_QK)),
            _const_spec((H_A, CHUNK, CHUNK)),
            _const_spec((CHUNK, D_A)),
            _const_spec((MXU_DIM, MXU_DIM)),
        ] + [pl.BlockSpec(memory_space=pl.ANY)] * len(prev),
        out_specs=tuple(out_specs),
        out_shape=tuple(out_shapes),
        input_output_aliases={N_PROJ_IN + j: o for j, o in enumerate(alias_out)},
        compiler_params=_params(("parallel",)),
        name="in_proj",
    )(x2d, g1, w_in_b, avg, qg, kg, ws, bias, bd, *prev)


def _lam(lq1_ref, lk1_ref, lq2_ref, lk2_ref, lam_init):
    s1 = jnp.sum(lq1_ref[...] * lk1_ref[...], axis=-1, keepdims=True)
    s2 = jnp.sum(lq2_ref[...] * lk2_ref[...], axis=-1, keepdims=True)
    return jnp.exp(s1) - jnp.exp(s2) + lam_init


def _out_norm(o, og, lam_init):
    return _rms(o) * og * (1.0 - lam_init)


def _attn_prompt_kernel(q_ref, k_ref, v_ref, lq1_ref, lk1_ref, lq2_ref, lk2_ref, og_ref, o_ref,
                        vt_ref, sa_ref, sb_ref, m_ref, acc_ref, *, lam_init, seq):
    qi = pl.program_id(2)

    @pl.when(qi == 0)
    def _():
        ones = jnp.ones((DVA - DV, TK), BF16)
        for j in range(seq // TK):
            vt_ref[j, 0:DV, :] = v_ref[j * TK:(j + 1) * TK, :].astype(F32).T.astype(BF16)
            vt_ref[j, DV:DVA, :] = ones

    q = q_ref[...]
    lane = lax.broadcasted_iota(jnp.int32, (TQ, LANES), 1)
    zero = jnp.zeros_like(q)
    qs = jnp.concatenate([jnp.where(lane < DK, q, zero), jnp.where(lane >= DK, q, zero)], axis=0)

    m_ref[...] = jnp.full(m_ref.shape, NEG_INF, F32)
    acc_ref[...] = jnp.zeros(acc_ref.shape, F32)

    def scores(kj, st_ref):
        start = pl.multiple_of(kj * TK, TK)
        st_ref[...] = _dot_nt(k_ref[pl.ds(start, TK), :], qs)

    def step(kj, st_ref, masked):
        def read():
            st = st_ref[...]
            if masked:
                c = lax.broadcasted_iota(jnp.int32, (TK, 2 * TQ), 1)
                qpos = qi * TQ + jnp.where(c >= TQ, c - TQ, c)
                kpos = kj * TK + lax.broadcasted_iota(jnp.int32, (TK, 2 * TQ), 0)
                st = jnp.where(kpos <= qpos, st, NEG_INF)
            return st

        m_prev = m_ref[...]
        m_new = jnp.maximum(m_prev, jnp.max(read(), axis=0, keepdims=True))
        alpha = jnp.exp2(m_prev - m_new)
        p = jnp.exp2(read() - m_new).astype(BF16)
        acc_ref[...] = alpha * acc_ref[...] + _dot(vt_ref[kj], p)
        m_ref[...] = m_new

    scores(0, sa_ref)

    def pair(i, carry):
        kj = 2 * i
        scores(kj + 1, sb_ref)
        step(kj, sa_ref, False)
        scores(kj + 2, sa_ref)
        step(kj + 1, sb_ref, False)
        return carry

    lax.fori_loop(0, lax.shift_right_logical(qi, 1), pair, 0)

    @pl.when((qi & 1) == 1)
    def _():
        scores(qi, sb_ref)
        step(qi - 1, sa_ref, False)
        step(qi, sb_ref, True)

    @pl.when((qi & 1) == 0)
    def _():
        step(qi, sa_ref, True)

    lam = _lam(lq1_ref, lk1_ref, lq2_ref, lk2_ref, lam_init)
    acc = acc_ref[...]
    o = acc[0:DV] / acc[DV:DV + 1]
    ot = o[:, :TQ] - lam * o[:, TQ:]
    o_ref[...] = _out_norm(ot.T, og_ref[...], lam_init).astype(BF16)


def _attn_prompt(q, kb, vb, lq1, lk1, lq2, lk2, og, lam_init, batch, seq):
    q3 = q.reshape(batch, seq, D_QK)
    k3 = kb.reshape(batch, seq, D_QK)
    v3 = vb.reshape(batch, seq, D_B)
    vec = _const_spec((1, DK))
    out = pl.pallas_call(
        functools.partial(_attn_prompt_kernel, lam_init=lam_init, seq=seq),
        grid=(batch, H_B, seq // TQ),
        in_specs=[
            pl.BlockSpec((None, TQ, LANES), lambda b, h, i: (b, i, h)),
            pl.BlockSpec((None, seq, LANES), lambda b, h, i: (b, 0, h)),
            pl.BlockSpec((None, seq, LANES), lambda b, h, i: (b, 0, h)),
            vec, vec, vec, vec,
            _const_spec((1, DV)),
        ],
        out_specs=pl.BlockSpec((None, TQ, LANES), lambda b, h, i: (b, i, h)),
        out_shape=jax.ShapeDtypeStruct((batch, seq, D_B), BF16),
        scratch_shapes=[
            pltpu.VMEM((seq // TK, DVA, TK), BF16),
            pltpu.VMEM((TK, 2 * TQ), F32),
            pltpu.VMEM((TK, 2 * TQ), F32),
            pltpu.VMEM((1, 2 * TQ), F32),
            pltpu.VMEM((DVA, 2 * TQ), F32),
        ],
        compiler_params=_params(("parallel", "parallel", "arbitrary")),
        name="attn_prompt",
    )(q3, k3, v3, lq1, lk1, lq2, lk2, og)
    return out.reshape(batch * seq, D_B)


def _attn_sample_kernel(pt_ref, q_ref, kn_ref, vn_ref, lq1_ref, lk1_ref, lq2_ref, lk2_ref, og_ref, *rest,
                        n_pages, n_new, lam_init):
    k_pages = rest[:n_pages]
    v_pages = rest[n_pages:2 * n_pages]
    o_ref, s_ref = rest[2 * n_pages:]
    n_rows = H_B * 2 * n_new
    n_cols = k_pages[0].shape[0]

    q = q_ref[...].astype(F32)
    lane = lax.broadcasted_iota(jnp.int32, (n_new, LANES), 1)
    parts = []
    for h in range(H_B):
        qh = q[:, h * LANES:(h + 1) * LANES]
        parts.append(jnp.where(lane < DK, qh, 0.0))
        parts.append(jnp.where(lane >= DK, qh, 0.0))
    qs = jnp.concatenate(parts, axis=0).astype(BF16)

    def valid_mask(cols, causal):
        r = lax.broadcasted_iota(jnp.int32, (n_rows, cols), 0)
        c = lax.broadcasted_iota(jnp.int32, (n_rows, cols), 1)
        valid = (c & (H_B - 1)) == lax.shift_right_logical(r, int(math.log2(2 * n_new)))
        if causal:
            valid = valid & (lax.shift_right_logical(c, int(math.log2(H_B))) <= (r & (n_new - 1)))
        return valid

    valid = valid_mask(n_cols, False)
    mx = jnp.full((n_rows, n_cols), NEG_INF, F32)
    for p in range(n_pages):
        s = jnp.where(valid, _dot_nt(qs, k_pages[p][...].astype(BF16)), NEG_INF)
        s_ref[:, p * n_cols:(p + 1) * n_cols] = s
        mx = jnp.maximum(mx, s)
    s_new = jnp.where(valid_mask(n_new * H_B, True), _dot_nt(qs, kn_ref[...]), NEG_INF)
    m = jnp.maximum(jnp.max(mx, axis=-1, keepdims=True), jnp.max(s_new, axis=-1, keepdims=True))

    p_new = jnp.exp2(s_new - m)
    acc = _dot(p_new.astype(BF16), vn_ref[...])
    lsum = jnp.zeros((n_rows, n_cols), F32)
    for p in range(n_pages):
        pp = jnp.exp2(s_ref[:, p * n_cols:(p + 1) * n_cols] - m)
        lsum = lsum + pp
        acc = acc + _dot(pp.astype(BF16), v_pages[p][...].astype(BF16))
    l = jnp.sum(lsum, axis=-1, keepdims=True) + jnp.sum(p_new, axis=-1, keepdims=True)

    lam = _lam(lq1_ref, lk1_ref, lq2_ref, lk2_ref, lam_init)
    o = acc / l
    for h in range(H_B):
        base = h * 2 * n_new
        oh = o[base:base + n_new] - lam * o[base + n_new:base + 2 * n_new]
        o_ref[:, h * DV:(h + 1) * DV] = _out_norm(oh, og_ref[...], lam_init)


def _attn_sample(q, kb, vb, cache_k, cache_v, page_table, layer, lq1, lk1, lq2, lk2, og, lam_init, dec_batch, n_new):
    n_pages = page_table.shape[1]
    page = cache_k.shape[2]
    n_pool = cache_k.shape[1]
    ck = cache_k.reshape(DEPTH, n_pool, page * H_B, 2 * DK)
    cv = cache_v.reshape(DEPTH, n_pool, page * H_B, DV)
    q3 = q.reshape(dec_batch, n_new, D_QK)
    kn = kb.reshape(dec_batch, n_new * H_B, 2 * DK)
    vn = vb.reshape(dec_batch, n_new * H_B, DV)
    n_rows = H_B * 2 * n_new
    vec = pl.BlockSpec((1, DK), lambda b, pt: (0, 0))

    def page_spec(p):
        return pl.BlockSpec((None, None, page * H_B, LANES), lambda b, pt: (layer, pt[b * n_pages + p], 0, 0))

    grid_spec = pltpu.PrefetchScalarGridSpec(
        num_scalar_prefetch=1,
        grid=(dec_batch,),
        in_specs=[
            pl.BlockSpec((None, n_new, D_QK), lambda b, pt: (b, 0, 0)),
            pl.BlockSpec((None, n_new * H_B, LANES), lambda b, pt: (b, 0, 0)),
            pl.BlockSpec((None, n_new * H_B, LANES), lambda b, pt: (b, 0, 0)),
            vec, vec, vec, vec,
            pl.BlockSpec((1, DV), lambda b, pt: (0, 0)),
        ] + [page_spec(p) for p in range(n_pages)] * 2,
        out_specs=pl.BlockSpec((None, n_new, D_B), lambda b, pt: (b, 0, 0)),
        scratch_shapes=[pltpu.VMEM((n_rows, n_pages * page * H_B), F32)],
    )
    out = pl.pallas_call(
        functools.partial(_attn_sample_kernel, n_pages=n_pages, n_new=n_new, lam_init=lam_init),
        grid_spec=grid_spec,
        out_shape=jax.ShapeDtypeStruct((dec_batch, n_new, D_B), F32),
        compiler_params=_params(("arbitrary",)),
        name="attn_sample",
    )(page_table.reshape(-1), q3, kn, vn, lq1, lk1, lq2, lk2, og, *([ck] * n_pages), *([cv] * n_pages))
    return out.reshape(dec_batch * n_new, D_B)


def _merge_router_kernel(*refs, grouped):
    a_ref, b_ref, x_ref, wo_ref, g2_ref, rhi_ref, rlo_ref = refs[:7]
    y = _dot(a_ref[...].astype(BF16), wo_ref[0:D_A, :]) + _dot(b_ref[...].astype(BF16), wo_ref[D_A:, :])
    x1 = x_ref[...] + y
    h = _rms(x1) * g2_ref[...]
    hb = h.astype(BF16)
    hlo = (h - hb.astype(F32)).astype(BF16)
    rhi = rhi_ref[...]
    logits = _dot(hb, rhi) + _dot(hlo, rhi) + _dot(hb, rlo_ref[...])

    lane = lax.broadcasted_iota(jnp.int32, logits.shape, 1)
    big = jnp.int32(ROUTER_LANES)

    def first_lane(mask):
        return jnp.min(jnp.where(mask, lane, big), axis=-1, keepdims=True)

    is_g = (lane >= GROUP_LANE0) & (lane < GROUP_LANE0 + N_GROUPS)
    lg = jnp.where(is_g, logits, NEG_INF)
    eg = jnp.where(is_g, jnp.exp(lg - jnp.max(lg, axis=-1, keepdims=True)), 0.0)
    pg = eg / jnp.sum(eg, axis=-1, keepdims=True)
    gw = jnp.max(pg, axis=-1, keepdims=True)
    gi = first_lane(is_g & (pg == gw)) - GROUP_LANE0
    sel = (lane < N_EXP) & (lax.shift_right_logical(lane, int(math.log2(E_PER))) == gi)
    le = jnp.where(sel, logits, NEG_INF)
    ee = jnp.where(sel, jnp.exp(le - jnp.max(le, axis=-1, keepdims=True)), 0.0)
    pe = ee / jnp.sum(ee, axis=-1, keepdims=True)
    t1 = jnp.max(jnp.where(sel, pe, -1.0), axis=-1, keepdims=True)
    i1 = first_lane(sel & (pe == t1))
    rest = sel & (lane != i1)
    t2 = jnp.max(jnp.where(rest, pe, -1.0), axis=-1, keepdims=True)
    i2 = first_lane(rest & (pe == t2))
    den = t1 + t2
    gates = jnp.where(lane == i1, gw * (t1 / den), jnp.where(lane == i2, gw * (t2 / den), 0.0))

    if not grouped:
        xg_ref, h_ref = refs[7:]
        h_ref[...] = hb
    else:
        ltri_ref, xg_ref, counts_ref, cnt_ref = refs[7:]

        @pl.when(pl.program_id(0) == 0)
        def _():
            cnt_ref[...] = jnp.zeros(cnt_ref.shape, F32)

        onehot = jnp.where(lane == gi, 1.0, 0.0)
        earlier = _dot(ltri_ref[...], onehot.astype(BF16))
        cnt = cnt_ref[...]
        rank = jnp.sum(onehot * (earlier + cnt), axis=-1, keepdims=True)
        cnt = cnt + jnp.sum(onehot, axis=0, keepdims=True)
        cnt_ref[...] = cnt
        counts_ref[...] = jnp.broadcast_to(cnt, counts_ref.shape)
        gates = gates + jnp.where(lane == META_GROUP_LANE, gi.astype(F32), jnp.where(lane == META_RANK_LANE, rank, 0.0))
    xg_ref[:, :D_MODEL] = x1
    xg_ref[:, D_MODEL:] = gates


def _merge_router(aout, bout, x2d, w_out_b, layer, g2, rhi, rlo, grouped):
    n = x2d.shape[0]
    tm = TM_PROJ
    row_spec = lambda w: pl.BlockSpec((tm, w), lambda i: (i, 0))
    in_specs = [
        row_spec(D_A), row_spec(D_B), row_spec(D_MODEL),
        pl.BlockSpec((None, D_MODEL, D_MODEL), lambda i: (layer, 0, 0)),
        _const_spec((1, D_MODEL)),
        _const_spec((D_MODEL, ROUTER_LANES)),
        _const_spec((D_MODEL, ROUTER_LANES)),
    ]
    args = [aout, bout, x2d, w_out_b, g2, rhi, rlo]
    xg_shape = jax.ShapeDtypeStruct((n, XG_W), F32)
    if grouped:
        r = jnp.arange(tm)
        args.append((r[:, None] > r[None, :]).astype(BF16))
        in_specs.append(_const_spec((tm, tm)))
        out_specs = (row_spec(XG_W), _const_spec((8, ROUTER_LANES)))
        out_shape = (xg_shape, jax.ShapeDtypeStruct((8, ROUTER_LANES), F32))
        scratch = [pltpu.VMEM((1, ROUTER_LANES), F32)]
    else:
        out_specs = (row_spec(XG_W), row_spec(D_MODEL))
        out_shape = (xg_shape, jax.ShapeDtypeStruct((n, D_MODEL), BF16))
        scratch = []
    return pl.pallas_call(
        functools.partial(_merge_router_kernel, grouped=grouped),
        grid=(n // tm,),
        in_specs=in_specs,
        out_specs=out_specs,
        out_shape=out_shape,
        scratch_shapes=scratch,
        compiler_params=_params(("arbitrary",)),
        name="merge_router",
    )(*args)


def _moe_kernel(h_ref, gates_ref, x1_ref, wg_ref, wu_ref, wd_ref, o_ref):
    e = pl.program_id(1)

    @pl.when(e == 0)
    def _():
        o_ref[...] = x1_ref[...]

    h = h_ref[...]
    gates = gates_ref[...]
    lane = lax.broadcasted_iota(jnp.int32, gates.shape, 1)
    ge = jnp.sum(jnp.where(lane == e, gates, 0.0), axis=-1, keepdims=True)
    a = jax.nn.silu(_dot(h, wg_ref[...])) * _dot(h, wu_ref[...]) * ge
    o_ref[...] += _dot(a.astype(BF16), wd_ref[...])


def _moe(h, xg, wg, wu, wd, layer):
    n = h.shape[0]
    tm = min(TM_MOE, n)
    return pl.pallas_call(
        _moe_kernel,
        grid=(n // tm, N_EXP),
        in_specs=[
            pl.BlockSpec((tm, D_MODEL), lambda i, e: (i, 0)),
            pl.BlockSpec((tm, ROUTER_LANES), lambda i, e: (i, D_MODEL // ROUTER_LANES)),
            pl.BlockSpec((tm, D_MODEL), lambda i, e: (i, 0)),
            pl.BlockSpec((None, None, D_MODEL, D_E), lambda i, e: (layer, e, 0, 0)),
            pl.BlockSpec((None, None, D_MODEL, D_E), lambda i, e: (layer, e, 0, 0)),
            pl.BlockSpec((None, None, D_E, D_MODEL), lambda i, e: (layer, e, 0, 0)),
        ],
        out_specs=pl.BlockSpec((tm, D_MODEL), lambda i, e: (i, 0)),
        out_shape=jax.ShapeDtypeStruct((n, D_MODEL), F32),
        compiler_params=_params(("parallel", "arbitrary")),
        name="moe",
    )(h, xg, xg, wg, wu, wd)


def _row_copies(n_rows, make_copy):
    def issue(r, carry):
        make_copy(r).start()
        return carry

    def drain(r, carry):
        make_copy(r).wait()
        return carry

    lax.fori_loop(0, n_rows, issue, 0, unroll=8)
    lax.fori_loop(0, n_rows, drain, 0, unroll=8)


def _dispatch_kernel(pos_ref, xg_ref, zeros_hbm, sorted_hbm, sem, *, tm):
    base = pl.program_id(0) * tm
    _row_copies(tm, lambda r: pltpu.make_async_copy(
        xg_ref.at[pl.ds(r, 1), :], sorted_hbm.at[pl.ds(pos_ref[base + r], 1), :], sem))


def _dispatch(pos, xg, n_pad):
    n = xg.shape[0]
    tm = TM_SORT
    return pl.pallas_call(
        functools.partial(_dispatch_kernel, tm=tm),
        grid_spec=pltpu.PrefetchScalarGridSpec(
            num_scalar_prefetch=1,
            grid=(n // tm,),
            in_specs=[pl.BlockSpec((tm, XG_W), lambda i, pos: (i, 0)), pl.BlockSpec(memory_space=pl.ANY)],
            out_specs=pl.BlockSpec(memory_space=pl.ANY),
            scratch_shapes=[pltpu.SemaphoreType.DMA(())],
        ),
        out_shape=jax.ShapeDtypeStruct((n_pad, XG_W), F32),
        input_output_aliases={2: 0},
        compiler_params=_params(("arbitrary",)),
        name="moe_dispatch",
    )(pos, xg, jnp.zeros((n_pad, XG_W), F32))


def _combine_kernel(pos_ref, sorted_hbm, o_ref, sem, *, tm):
    base = pl.program_id(0) * tm
    _row_copies(tm, lambda r: pltpu.make_async_copy(
        sorted_hbm.at[pl.ds(pos_ref[base + r], 1), :], o_ref.at[pl.ds(r, 1), :], sem))


def _combine(pos, y_sorted, n):
    tm = TM_SORT
    return pl.pallas_call(
        functools.partial(_combine_kernel, tm=tm),
        grid_spec=pltpu.PrefetchScalarGridSpec(
            num_scalar_prefetch=1,
            grid=(n // tm,),
            in_specs=[pl.BlockSpec(memory_space=pl.ANY)],
            out_specs=pl.BlockSpec((tm, D_MODEL), lambda i, pos: (i, 0)),
            scratch_shapes=[pltpu.SemaphoreType.DMA(())],
        ),
        out_shape=jax.ShapeDtypeStruct((n, D_MODEL), F32),
        compiler_params=_params(("arbitrary",)),
        name="moe_combine",
    )(pos, y_sorted)


def _moe_sorted_kernel(tg_ref, nv_ref, xg_ref, g2_ref, wg_ref, wu_ref, wd_ref, o_ref):
    j = pl.program_id(0)

    @pl.when(j >= nv_ref[0])
    def _():
        o_ref[...] = jnp.zeros(o_ref.shape, F32)

    @pl.when(j < nv_ref[0])
    def _():
        first_expert = tg_ref[j] * E_PER
        x1 = xg_ref[:, :D_MODEL]
        gates = xg_ref[:, D_MODEL:]
        hb = (_rms(x1) * g2_ref[...]).astype(BF16)
        lane = lax.broadcasted_iota(jnp.int32, gates.shape, 1)
        acc = x1
        for e in range(E_PER):
            ge = jnp.sum(jnp.where(lane == first_expert + e, gates, 0.0), axis=-1, keepdims=True)
            a = jax.nn.silu(_dot(hb, wg_ref[e])) * _dot(hb, wu_ref[e]) * ge
            acc = acc + _dot(a.astype(BF16), wd_ref[e])
        o_ref[...] = acc


def _moe_sorted(tile_group, n_valid, xg_sorted, g2, wg, wu, wd, layer):
    n_pad = xg_sorted.shape[0]
    tm = TM_SORT
    by_group = lambda w: w.reshape(DEPTH, N_GROUPS, E_PER, *w.shape[2:])
    last = lambda j, nv: jnp.minimum(j, nv[0] - 1)
    w_spec = lambda a, b: pl.BlockSpec((None, None, E_PER, a, b), lambda j, tg, nv: (layer, tg[last(j, nv)], 0, 0, 0))
    return pl.pallas_call(
        _moe_sorted_kernel,
        grid_spec=pltpu.PrefetchScalarGridSpec(
            num_scalar_prefetch=2,
            grid=(n_pad // tm,),
            in_specs=[
                pl.BlockSpec((tm, XG_W), lambda j, tg, nv: (last(j, nv), 0)),
                pl.BlockSpec((1, D_MODEL), lambda j, tg, nv: (0, 0)),
                w_spec(D_MODEL, D_E), w_spec(D_MODEL, D_E), w_spec(D_E, D_MODEL),
            ],
            out_specs=pl.BlockSpec((tm, D_MODEL), lambda j, tg, nv: (j, 0)),
        ),
        out_shape=jax.ShapeDtypeStruct((n_pad, D_MODEL), F32),
        compiler_params=_params(("arbitrary",)),
        name="moe_sorted",
    )(tile_group, n_valid, xg_sorted, g2, by_group(wg), by_group(wu), by_group(wd))


def _moe_grouped(xg, counts, g2, wg, wu, wd, layer):
    n = xg.shape[0]
    n_pad = n + N_GROUPS * TM_SORT
    n_tiles = n_pad // TM_SORT
    meta = xg[:, D_MODEL + META_GROUP_LANE:D_MODEL + META_RANK_LANE + 1].astype(jnp.int32)
    tiles_per_group = (counts[0, :N_GROUPS].astype(jnp.int32) + TM_SORT - 1) // TM_SORT
    tile_end = jnp.cumsum(tiles_per_group)
    pos = ((tile_end - tiles_per_group) * TM_SORT)[meta[:, 0]] + meta[:, 1]
    tile_group = jnp.minimum(jnp.sum(jnp.arange(n_tiles)[:, None] >= tile_end[None, :], axis=1), N_GROUPS - 1)
    xg_sorted = _dispatch(pos, xg, n_pad)
    y_sorted = _moe_sorted(tile_group.astype(jnp.int32), tile_end[N_GROUPS - 1:], xg_sorted, g2, wg, wu, wd, layer)
    return _combine(pos, y_sorted, n)


def _router_weights(r_g, r_e):
    r = jnp.concatenate([r_e, r_g, jnp.zeros((D_MODEL, ROUTER_LANES - N_EXP - N_GROUPS), F32)], axis=1)
    hi = r.astype(BF16)
    lo = (r - hi.astype(F32)).astype(BF16)
    return hi, lo


def kernel(x_prompt, x_sample, cache_k, cache_v, page_table, norm1_g, w_in, a_vnorm_g, a_ws, a_bs, qn_g, kn_g, lam_q1, lam_k1, lam_q2, lam_k2, b_onorm_g, w_out, norm2_g, router_g, router_e, w_gate, w_up, w_down):
    batch, seq, _ = x_prompt.shape
    dec_batch, n_new, _ = x_sample.shape
    xp = x_prompt.reshape(batch * seq, D_MODEL)
    xs = x_sample.reshape(dec_batch * n_new, D_MODEL)

    ii = jnp.arange(MXU_DIM) // DK
    bd = (ii[:, None] == ii[None, :]).astype(BF16)
    eye = jnp.eye(CHUNK // n_new, dtype=F32)

    w_in_b, w_out_b = w_in.astype(BF16), w_out.astype(BF16)
    wg, wu, wd = w_gate.astype(BF16), w_up.astype(BF16), w_down.astype(BF16)

    rows_p, rows_s = None, None
    for l in range(DEPTH):
        lam_init = 0.8 - 0.6 * math.exp(-0.3 * l)
        rhi, rlo = _router_weights(router_g[l], router_e[l])
        g1 = norm1_g[l][None]
        g2 = norm2_g[l][None]
        avg = a_vnorm_g[l][None]
        qg = jnp.tile(qn_g[l], D_QK // DK)[None]
        kg = jnp.tile(kn_g[l], D_QK // DK)[None]
        og = b_onorm_g[l][None]
        lams = (lam_q1[l][None], lam_k1[l][None], lam_q2[l][None], lam_k2[l][None])
        ws_p = a_ws[l]
        bias_p = jnp.repeat(a_bs[l].T, DH_A, axis=1)
        ws_s = jax.vmap(lambda w: jnp.kron(eye, w[:n_new, :n_new]))(a_ws[l])
        bias_s = jnp.tile(jnp.repeat(a_bs[l][:, :n_new].T, DH_A, axis=1), (CHUNK // n_new, 1))

        aout, q, k_all, kb, v_all, vb = _in_proj(xp, g1, w_in_b, avg, qg, kg, ws_p, bias_p, bd, l, rows_p, False)
        rows_p = (k_all, v_all)
        bout = _attn_prompt(q, kb, vb, *lams, og, lam_init, batch, seq)
        xg, counts = _merge_router(aout, bout, xp, w_out_b, l, g2, rhi, rlo, True)
        xp = _moe_grouped(xg, counts, g2, wg, wu, wd, l)

        aout, q, k_all, kb, v_all, vb, va_all = _in_proj(xs, g1, w_in_b, avg, qg, kg, ws_s, bias_s, bd, l, rows_s, True)
        rows_s = (k_all, v_all, va_all)
        bout = _attn_sample(q, kb, vb, cache_k, cache_v, page_table, l, *lams, og, lam_init, dec_batch, n_new)
        xg, h = _merge_router(aout, bout, xs, w_out_b, l, g2, rhi, rlo, False)
        xs = _moe(h, xg, wg, wu, wd, l)

    return (xp.reshape(batch, seq, D_MODEL), xs.reshape(dec_batch, n_new, D_MODEL),
            rows_p[0].reshape(DEPTH, batch, seq, H_B, 2 * DK), rows_p[1].reshape(DEPTH, batch, seq, H_B, DV),
            rows_s[0].reshape(DEPTH, dec_batch, n_new, H_B, 2 * DK), rows_s[1].reshape(DEPTH, dec_batch, n_new, H_B, DV),
            rows_s[2].reshape(DEPTH, dec_batch, n_new, D_A))
```

```python
import functools
import math

import jax
import jax.numpy as jnp
from jax import lax
from jax.experimental import pallas as pl
from jax.experimental.pallas import tpu as pltpu

F32 = jnp.float32
BF16 = jnp.bfloat16

D_MODEL = 1024
DEPTH = 2
D_A = 512
H_A = 8
DH_A = D_A // H_A
CHUNK = 128
D_B = 512
H_B = 4
DV = D_B // H_B
DK = DV // 2
D_QK = H_B * 2 * DK
D_IN = 2 * D_A + 2 * D_QK + D_B
ATTN_SCALE = DK ** -0.5
LOG2E = math.log2(math.e)
DVA = DV + 16
NEG_INF = -1e30
N_GROUPS = 4
E_PER = 4
N_EXP = N_GROUPS * E_PER
D_E = 256
EPS = 1e-6

LANES = 128
MXU_DIM = 256
VMEM_LIMIT_BYTES = 48 * 1024 * 1024

TM_PROJ = 512
TQ = 512
TK = 512
assert TQ == TK
TM_MOE = 1024
ROUTER_LANES = 128
GROUP_LANE0 = N_EXP
XG_W = D_MODEL + ROUTER_LANES
META_GROUP_LANE = N_EXP
META_RANK_LANE = N_EXP + 1
TM_SORT = 512
ROUTER_ROWS = 24


def _params(sem):
    return pltpu.CompilerParams(dimension_semantics=sem, vmem_limit_bytes=VMEM_LIMIT_BYTES)


def _const_spec(shape):
    nd = len(shape)
    return pl.BlockSpec(shape, lambda *_: (0,) * nd)


def _rms(x):
    return x * lax.rsqrt(jnp.mean(x * x, axis=-1, keepdims=True) + EPS)


def _dot(a, b):
    return jnp.dot(a, b, preferred_element_type=F32)


def _dot_nt(a, b):
    return lax.dot_general(a, b, (((1,), (1,)), ((), ())), preferred_element_type=F32)


N_PROJ_IN = 9


def _in_proj_kernel(*refs, tm, n_alias, emit_va):
    x_ref, g1_ref, w_ref, avg_ref, qg_ref, kg_ref, ws_ref, bias_ref, bd_ref = refs[:N_PROJ_IN]
    outs = refs[N_PROJ_IN + n_alias:]
    aout_ref, q_ref, k32_ref, kb_ref, v32_ref, vb_ref = outs[:6]
    x = x_ref[...]
    xb = (_rms(x) * g1_ref[...]).astype(BF16)

    a = jax.nn.gelu(_dot(xb, w_ref[:, 0:2 * D_A]))
    u = a[:, :D_A]
    va = _rms(a[:, D_A:]) * avg_ref[...]
    if emit_va:
        outs[6][...] = va
    vab = va.astype(BF16)

    row = lax.broadcasted_iota(jnp.int32, (CHUNK, CHUNK), 0)
    col = lax.broadcasted_iota(jnp.int32, (CHUNK, CHUNK), 1)
    tril = row >= col
    wsm = [jnp.where(tril, ws_ref[h], 0.0).astype(BF16) for h in range(H_A)]
    first_head = lax.broadcasted_iota(jnp.int32, (CHUNK, LANES), 1) < DH_A
    for c in range(tm // CHUNK):
        rows = slice(c * CHUNK, (c + 1) * CHUNK)
        for hp in range(H_A // 2):
            cols = slice(hp * LANES, (hp + 1) * LANES)
            blk = vab[rows, cols]
            s = jnp.where(first_head, _dot(wsm[2 * hp], blk), _dot(wsm[2 * hp + 1], blk))
            s = s + bias_ref[:, cols]
            aout_ref[rows, cols] = (u[rows, cols] * s).astype(BF16)

    bd = bd_ref[...]

    def group_norm(z, g):
        sq = (z * z).astype(BF16)
        ss = jnp.concatenate([_dot(sq[:, i * MXU_DIM:(i + 1) * MXU_DIM], bd) for i in range(D_QK // MXU_DIM)], axis=1)
        return z * lax.rsqrt(ss * (1.0 / DK) + EPS) * g

    qn = group_norm(_dot(xb, w_ref[:, 2 * D_A:2 * D_A + D_QK]), qg_ref[...])
    q_ref[...] = (qn * (ATTN_SCALE * LOG2E)).astype(BF16)
    kn = group_norm(_dot(xb, w_ref[:, 2 * D_A + D_QK:2 * D_A + 2 * D_QK]), kg_ref[...])
    kb_ref[...] = kn.astype(BF16)
    v = _dot(xb, w_ref[:, 2 * D_A + 2 * D_QK:D_IN])
    vb_ref[...] = v.astype(BF16)
    for h in range(H_B):
        k32_ref[pl.ds(h, tm, stride=H_B), :] = kn[:, h * LANES:(h + 1) * LANES]
        v32_ref[pl.ds(h, tm, stride=H_B), :] = v[:, h * LANES:(h + 1) * LANES]


def _in_proj(x2d, g1, w_in_b, avg, qg, kg, ws, bias, bd, layer, prev, emit_va):
    n = x2d.shape[0]
    tm = TM_PROJ
    row_spec = lambda w: pl.BlockSpec((tm, w), lambda i: (i, 0))
    rows4 = jax.ShapeDtypeStruct((DEPTH, n * H_B, LANES), F32)
    rows4_spec = pl.BlockSpec((None, tm * H_B, LANES), lambda i: (layer, i, 0))
    out_shapes = [
        jax.ShapeDtypeStruct((n, D_A), BF16),
        jax.ShapeDtypeStruct((n, D_QK), BF16),
        rows4,
        jax.ShapeDtypeStruct((n, D_QK), BF16),
        rows4,
        jax.ShapeDtypeStruct((n, D_B), BF16),
    ]
    out_specs = [row_spec(D_A), row_spec(D_QK), rows4_spec, row_spec(D_QK), rows4_spec, row_spec(D_B)]
    if emit_va:
        out_shapes.append(jax.ShapeDtypeStruct((DEPTH, n, D_A), F32))
        out_specs.append(pl.BlockSpec((None, tm, D_A), lambda i: (layer, i, 0)))
    prev = () if prev is None else tuple(prev)
    alias_out = (2, 4, 6)[:len(prev)]
    return pl.pallas_call(
        functools.partial(_in_proj_kernel, tm=tm, n_alias=len(prev), emit_va=emit_va),
        grid=(n // tm,),
        in_specs=[
            row_spec(D_MODEL),
            _const_spec((1, D_MODEL)),
            pl.BlockSpec((None, D_MODEL, D_IN), lambda i: (layer, 0, 0)),
            _const_spec((1, D_A)),
            _const_spec((1, D_QK)),
            _const_spec((1, D_QK)),
            _const_spec((H_A, CHUNK, CHUNK)),
            _const_spec((CHUNK, D_A)),
            _const_spec((MXU_DIM, MXU_DIM)),
        ] + [pl.BlockSpec(memory_space=pl.ANY)] * len(prev),
        out_specs=tuple(out_specs),
        out_shape=tuple(out_shapes),
        input_output_aliases={N_PROJ_IN + j: o for j, o in enumerate(alias_out)},
        compiler_params=_params(("parallel",)),
        name="in_proj",
    )(x2d, g1, w_in_b, avg, qg, kg, ws, bias, bd, *prev)


def _lam(lq1_ref, lk1_ref, lq2_ref, lk2_ref, lam_init):
    s1 = jnp.sum(lq1_ref[...] * lk1_ref[...], axis=-1, keepdims=True)
    s2 = jnp.sum(lq2_ref[...] * lk2_ref[...], axis=-1, keepdims=True)
    return jnp.exp(s1) - jnp.exp(s2) + lam_init


def _out_norm(o, og, lam_init):
    return _rms(o) * og * (1.0 - lam_init)


def _attn_prompt_kernel(q_ref, k_ref, v_ref, lq1_ref, lk1_ref, lq2_ref, lk2_ref, og_ref, o_ref,
                        vt_ref, sa_ref, sb_ref, m_ref, acc_ref, *, lam_init, seq):
    qi = pl.program_id(2)

    @pl.when(qi == 0)
    def _():
        ones = jnp.ones((DVA - DV, TK), BF16)
        for j in range(seq // TK):
            vt_ref[j, 0:DV, :] = v_ref[j * TK:(j + 1) * TK, :].astype(F32).T.astype(BF16)
            vt_ref[j, DV:DVA, :] = ones

    q = q_ref[...]
    lane = lax.broadcasted_iota(jnp.int32, (TQ, LANES), 1)
    zero = jnp.zeros_like(q)
    qs = jnp.concatenate([jnp.where(lane < DK, q, zero), jnp.where(lane >= DK, q, zero)], axis=0)

    m_ref[...] = jnp.full(m_ref.shape, NEG_INF, F32)
    acc_ref[...] = jnp.zeros(acc_ref.shape, F32)

    def scores(kj, st_ref):
        start = pl.multiple_of(kj * TK, TK)
        st_ref[...] = _dot_nt(k_ref[pl.ds(start, TK), :], qs)

    def step(kj, st_ref, diagonal):
        if diagonal:
            r = lax.broadcasted_iota(jnp.int32, (TK, 2 * TQ), 0)
            c = lax.broadcasted_iota(jnp.int32, (TK, 2 * TQ), 1)
            st_ref[...] = jnp.where(r <= jnp.where(c >= TQ, c - TQ, c), st_ref[...], NEG_INF)
        m_prev = m_ref[...]
        m_new = jnp.maximum(m_prev, jnp.max(st_ref[...], axis=0, keepdims=True))
        alpha = jnp.exp2(m_prev - m_new)
        p = jnp.exp2(st_ref[...] - m_new).astype(BF16)
        acc_ref[...] = alpha * acc_ref[...] + _dot(vt_ref[kj], p)
        m_ref[...] = m_new

    scores(0, sa_ref)

    def pair(i, carry):
        kj = 2 * i
        scores(kj + 1, sb_ref)
        step(kj, sa_ref, False)
        scores(kj + 2, sa_ref)
        step(kj + 1, sb_ref, False)
        return carry

    lax.fori_loop(0, lax.shift_right_logical(qi, 1), pair, 0)

    @pl.when((qi & 1) == 1)
    def _():
        scores(qi, sb_ref)
        step(qi - 1, sa_ref, False)
        step(qi, sb_ref, True)

    @pl.when((qi & 1) == 0)
    def _():
        step(qi, sa_ref, True)

    lam = _lam(lq1_ref, lk1_ref, lq2_ref, lk2_ref, lam_init)
    acc = acc_ref[...]
    o = acc[0:DV] / acc[DV:DV + 1]
    ot = o[:, :TQ] - lam * o[:, TQ:]
    o_ref[...] = _out_norm(ot.T, og_ref[...], lam_init).astype(BF16)


def _attn_prompt(q, kb, vb, lq1, lk1, lq2, lk2, og, lam_init, batch, seq):
    q3 = q.reshape(batch, seq, D_QK)
    k3 = kb.reshape(batch, seq, D_QK)
    v3 = vb.reshape(batch, seq, D_B)
    vec = _const_spec((1, DK))
    out = pl.pallas_call(
        functools.partial(_attn_prompt_kernel, lam_init=lam_init, seq=seq),
        grid=(batch, H_B, seq // TQ),
        in_specs=[
            pl.BlockSpec((None, TQ, LANES), lambda b, h, i: (b, i, h)),
            pl.BlockSpec((None, seq, LANES), lambda b, h, i: (b, 0, h)),
            pl.BlockSpec((None, seq, LANES), lambda b, h, i: (b, 0, h)),
            vec, vec, vec, vec,
            _const_spec((1, DV)),
        ],
        out_specs=pl.BlockSpec((None, TQ, LANES), lambda b, h, i: (b, i, h)),
        out_shape=jax.ShapeDtypeStruct((batch, seq, D_B), BF16),
        scratch_shapes=[
            pltpu.VMEM((seq // TK, DVA, TK), BF16),
            pltpu.VMEM((TK, 2 * TQ), F32),
            pltpu.VMEM((TK, 2 * TQ), F32),
            pltpu.VMEM((1, 2 * TQ), F32),
            pltpu.VMEM((DVA, 2 * TQ), F32),
        ],
        compiler_params=_params(("parallel", "parallel", "arbitrary")),
        name="attn_prompt",
    )(q3, k3, v3, lq1, lk1, lq2, lk2, og)
    return out.reshape(batch * seq, D_B)


def _attn_sample_kernel(pt_ref, q_ref, kn_ref, vn_ref, lq1_ref, lk1_ref, lq2_ref, lk2_ref, og_ref, *rest,
                        n_pages, n_new, lam_init):
    k_pages = rest[:n_pages]
    v_pages = rest[n_pages:2 * n_pages]
    o_ref, s_ref = rest[2 * n_pages:]
    n_rows = H_B * 2 * n_new
    n_cols = k_pages[0].shape[0]

    q = q_ref[...].astype(F32)
    lane = lax.broadcasted_iota(jnp.int32, (n_new, LANES), 1)
    parts = []
    for h in range(H_B):
        qh = q[:, h * LANES:(h + 1) * LANES]
        parts.append(jnp.where(lane < DK, qh, 0.0))
        parts.append(jnp.where(lane >= DK, qh, 0.0))
    qs = jnp.concatenate(parts, axis=0).astype(BF16)

    def valid_mask(cols, causal):
        r = lax.broadcasted_iota(jnp.int32, (n_rows, cols), 0)
        c = lax.broadcasted_iota(jnp.int32, (n_rows, cols), 1)
        valid = (c & (H_B - 1)) == lax.shift_right_logical(r, int(math.log2(2 * n_new)))
        if causal:
            valid = valid & (lax.shift_right_logical(c, int(math.log2(H_B))) <= (r & (n_new - 1)))
        return valid

    valid = valid_mask(n_cols, False)
    mx = jnp.full((n_rows, n_cols), NEG_INF, F32)
    for p in range(n_pages):
        s = jnp.where(valid, _dot_nt(qs, k_pages[p][...].astype(BF16)), NEG_INF)
        s_ref[:, p * n_cols:(p + 1) * n_cols] = s
        mx = jnp.maximum(mx, s)
    s_new = jnp.where(valid_mask(n_new * H_B, True), _dot_nt(qs, kn_ref[...]), NEG_INF)
    m = jnp.maximum(jnp.max(mx, axis=-1, keepdims=True), jnp.max(s_new, axis=-1, keepdims=True))

    p_new = jnp.exp2(s_new - m)
    acc = _dot(p_new.astype(BF16), vn_ref[...])
    lsum = jnp.zeros((n_rows, n_cols), F32)
    for p in range(n_pages):
        pp = jnp.exp2(s_ref[:, p * n_cols:(p + 1) * n_cols] - m)
        lsum = lsum + pp
        acc = acc + _dot(pp.astype(BF16), v_pages[p][...].astype(BF16))
    l = jnp.sum(lsum, axis=-1, keepdims=True) + jnp.sum(p_new, axis=-1, keepdims=True)

    lam = _lam(lq1_ref, lk1_ref, lq2_ref, lk2_ref, lam_init)
    o = acc / l
    for h in range(H_B):
        base = h * 2 * n_new
        oh = o[base:base + n_new] - lam * o[base + n_new:base + 2 * n_new]
        o_ref[:, h * DV:(h + 1) * DV] = _out_norm(oh, og_ref[...], lam_init)


def _attn_sample(q, kb, vb, cache_k, cache_v, page_table, layer, lq1, lk1, lq2, lk2, og, lam_init, dec_batch, n_new):
    n_pages = page_table.shape[1]
    page = cache_k.shape[2]
    n_pool = cache_k.shape[1]
    ck = cache_k.reshape(DEPTH, n_pool, page * H_B, 2 * DK)
    cv = cache_v.reshape(DEPTH, n_pool, page * H_B, DV)
    q3 = q.reshape(dec_batch, n_new, D_QK)
    kn = kb.reshape(dec_batch, n_new * H_B, 2 * DK)
    vn = vb.reshape(dec_batch, n_new * H_B, DV)
    n_rows = H_B * 2 * n_new
    vec = pl.BlockSpec((1, DK), lambda b, pt: (0, 0))

    def page_spec(p):
        return pl.BlockSpec((None, None, page * H_B, LANES), lambda b, pt: (layer, pt[b * n_pages + p], 0, 0))

    grid_spec = pltpu.PrefetchScalarGridSpec(
        num_scalar_prefetch=1,
        grid=(dec_batch,),
        in_specs=[
            pl.BlockSpec((None, n_new, D_QK), lambda b, pt: (b, 0, 0)),
            pl.BlockSpec((None, n_new * H_B, LANES), lambda b, pt: (b, 0, 0)),
            pl.BlockSpec((None, n_new * H_B, LANES), lambda b, pt: (b, 0, 0)),
            vec, vec, vec, vec,
            pl.BlockSpec((1, DV), lambda b, pt: (0, 0)),
        ] + [page_spec(p) for p in range(n_pages)] * 2,
        out_specs=pl.BlockSpec((None, n_new, D_B), lambda b, pt: (b, 0, 0)),
        scratch_shapes=[pltpu.VMEM((n_rows, n_pages * page * H_B), F32)],
    )
    out = pl.pallas_call(
        functools.partial(_attn_sample_kernel, n_pages=n_pages, n_new=n_new, lam_init=lam_init),
        grid_spec=grid_spec,
        out_shape=jax.ShapeDtypeStruct((dec_batch, n_new, D_B), F32),
        compiler_params=_params(("arbitrary",)),
        name="attn_sample",
    )(page_table.reshape(-1), q3, kn, vn, lq1, lk1, lq2, lk2, og, *([ck] * n_pages), *([cv] * n_pages))
    return out.reshape(dec_batch * n_new, D_B)


def _merge_router_kernel(*refs, grouped):
    a_ref, b_ref, x_ref, wo_ref, g2_ref, rhi_ref, rlo_ref = refs[:7]
    y = _dot(a_ref[...].astype(BF16), wo_ref[0:D_A, :]) + _dot(b_ref[...].astype(BF16), wo_ref[D_A:, :])
    x1 = x_ref[...] + y
    h = _rms(x1) * g2_ref[...]
    hb = h.astype(BF16)
    hlo = (h - hb.astype(F32)).astype(BF16)
    rhi = rhi_ref[...]
    logits = _dot(hb, rhi) + _dot(hlo, rhi) + _dot(hb, rlo_ref[...])

    lt = logits.T[0:ROUTER_ROWS, :]
    row_i = lax.broadcasted_iota(jnp.int32, lt.shape, 0)
    row = row_i.astype(F32)
    row_group = lax.shift_right_logical(row_i, int(math.log2(E_PER))).astype(F32)
    big = float(ROUTER_ROWS)

    def first_row(mask):
        return jnp.min(jnp.where(mask, row, big), axis=0, keepdims=True)

    is_g = (row_i >= GROUP_LANE0) & (row_i < GROUP_LANE0 + N_GROUPS)
    lg = jnp.where(is_g, lt, NEG_INF)
    eg = jnp.where(is_g, jnp.exp(lg - jnp.max(lg, axis=0, keepdims=True)), 0.0)
    pg = eg / jnp.sum(eg, axis=0, keepdims=True)
    gw = jnp.max(pg, axis=0, keepdims=True)
    gi = first_row(is_g & (pg == gw)) - GROUP_LANE0
    sel = (row_i < N_EXP) & (row_group == gi)
    le = jnp.where(sel, lt, NEG_INF)
    ee = jnp.where(sel, jnp.exp(le - jnp.max(le, axis=0, keepdims=True)), 0.0)
    pe = ee / jnp.sum(ee, axis=0, keepdims=True)
    t1 = jnp.max(jnp.where(sel, pe, -1.0), axis=0, keepdims=True)
    i1 = first_row(sel & (pe == t1))
    rest = sel & (row != i1)
    t2 = jnp.max(jnp.where(rest, pe, -1.0), axis=0, keepdims=True)
    i2 = first_row(rest & (pe == t2))
    den = t1 + t2
    gates_t = jnp.where(row == i1, gw * (t1 / den), jnp.where(row == i2, gw * (t2 / den), 0.0))

    if not grouped:
        xg_ref, h_ref = refs[7:]
        h_ref[...] = hb
    else:
        utri_ref, xg_ref, counts_ref, cnt_ref = refs[7:]

        @pl.when(pl.program_id(0) == 0)
        def _():
            cnt_ref[...] = jnp.zeros(cnt_ref.shape, F32)

        row8 = lax.broadcasted_iota(jnp.int32, (SUBLANES, lt.shape[1]), 0).astype(F32)
        onehot = jnp.where(row8 == gi, 1.0, 0.0)
        earlier = _dot(onehot.astype(BF16), utri_ref[...])
        cnt = cnt_ref[...]
        rank = jnp.sum(onehot * (earlier + cnt), axis=0, keepdims=True)
        cnt = cnt + jnp.sum(onehot, axis=1, keepdims=True)
        cnt_ref[...] = cnt
        counts_ref[...] = cnt[:, :ROUTER_LANES]
        gates_t = gates_t + jnp.where(row == META_GROUP_LANE, gi, jnp.where(row == META_RANK_LANE, rank, 0.0))
    pad = jnp.zeros((ROUTER_LANES - ROUTER_ROWS, gates_t.shape[1]), F32)
    xg_ref[:, :D_MODEL] = x1
    xg_ref[:, D_MODEL:] = jnp.concatenate([gates_t, pad], axis=0).T


def _merge_router(aout, bout, x2d, w_out_b, layer, g2, rhi, rlo, grouped):
    n = x2d.shape[0]
    tm = TM_PROJ
    row_spec = lambda w: pl.BlockSpec((tm, w), lambda i: (i, 0))
    in_specs = [
        row_spec(D_A), row_spec(D_B), row_spec(D_MODEL),
        pl.BlockSpec((None, D_MODEL, D_MODEL), lambda i: (layer, 0, 0)),
        _const_spec((1, D_MODEL)),
        _const_spec((D_MODEL, ROUTER_LANES)),
        _const_spec((D_MODEL, ROUTER_LANES)),
    ]
    args = [aout, bout, x2d, w_out_b, g2, rhi, rlo]
    xg_shape = jax.ShapeDtypeStruct((n, XG_W), F32)
    if grouped:
        r = jnp.arange(tm)
        args.append((r[:, None] < r[None, :]).astype(BF16))
        in_specs.append(_const_spec((tm, tm)))
        out_specs = (row_spec(XG_W), _const_spec((8, ROUTER_LANES)))
        out_shape = (xg_shape, jax.ShapeDtypeStruct((8, ROUTER_LANES), F32))
        scratch = [pltpu.VMEM((8, tm), F32)]
    else:
        out_specs = (row_spec(XG_W), row_spec(D_MODEL))
        out_shape = (xg_shape, jax.ShapeDtypeStruct((n, D_MODEL), BF16))
        scratch = []
    return pl.pallas_call(
        functools.partial(_merge_router_kernel, grouped=grouped),
        grid=(n // tm,),
        in_specs=in_specs,
        out_specs=out_specs,
        out_shape=out_shape,
        scratch_shapes=scratch,
        compiler_params=_params(("arbitrary",)),
        name="merge_router",
    )(*args)


def _moe_kernel(h_ref, gates_ref, x1_ref, wg_ref, wu_ref, wd_ref, o_ref):
    e = pl.program_id(1)

    @pl.when(e == 0)
    def _():
        o_ref[...] = x1_ref[...]

    h = h_ref[...]
    gates = gates_ref[...]
    lane = lax.broadcasted_iota(jnp.int32, gates.shape, 1)
    ge = jnp.sum(jnp.where(lane == e, gates, 0.0), axis=-1, keepdims=True)
    a = jax.nn.silu(_dot(h, wg_ref[...])) * _dot(h, wu_ref[...]) * ge
    o_ref[...] += _dot(a.astype(BF16), wd_ref[...])


def _moe(h, xg, wg, wu, wd, layer):
    n = h.shape[0]
    tm = min(TM_MOE, n)
    return pl.pallas_call(
        _moe_kernel,
        grid=(n // tm, N_EXP),
        in_specs=[
            pl.BlockSpec((tm, D_MODEL), lambda i, e: (i, 0)),
            pl.BlockSpec((tm, ROUTER_LANES), lambda i, e: (i, D_MODEL // ROUTER_LANES)),
            pl.BlockSpec((tm, D_MODEL), lambda i, e: (i, 0)),
            pl.BlockSpec((None, None, D_MODEL, D_E), lambda i, e: (layer, e, 0, 0)),
            pl.BlockSpec((None, None, D_MODEL, D_E), lambda i, e: (layer, e, 0, 0)),
            pl.BlockSpec((None, None, D_E, D_MODEL), lambda i, e: (layer, e, 0, 0)),
        ],
        out_specs=pl.BlockSpec((tm, D_MODEL), lambda i, e: (i, 0)),
        out_shape=jax.ShapeDtypeStruct((n, D_MODEL), F32),
        compiler_params=_params(("parallel", "arbitrary")),
        name="moe",
    )(h, xg, xg, wg, wu, wd)


SUBLANES = 8


def _row_copies(n_rows, make_copy):
    def issue(k, carry):
        r0 = pl.multiple_of(k * SUBLANES, SUBLANES)
        for u in range(SUBLANES):
            make_copy(r0, u).start()
        return carry

    def drain(k, carry):
        r0 = pl.multiple_of(k * SUBLANES, SUBLANES)
        for u in range(SUBLANES):
            make_copy(r0, u).wait()
        return carry

    lax.fori_loop(0, n_rows // SUBLANES, issue, 0)
    lax.fori_loop(0, n_rows // SUBLANES, drain, 0)


def _dispatch_kernel(pos_ref, xg_ref, zeros_hbm, sorted_hbm, sem, *, tm):
    base = pl.program_id(0) * tm
    _row_copies(tm, lambda r0, u: pltpu.make_async_copy(
        xg_ref.at[pl.ds(r0, SUBLANES), :].at[pl.ds(u, 1), :],
        sorted_hbm.at[pl.ds(pos_ref[base + r0 + u], 1), :], sem))


def _dispatch(pos, xg, n_pad):
    n = xg.shape[0]
    tm = TM_SORT
    return pl.pallas_call(
        functools.partial(_dispatch_kernel, tm=tm),
        grid_spec=pltpu.PrefetchScalarGridSpec(
            num_scalar_prefetch=1,
            grid=(n // tm,),
            in_specs=[pl.BlockSpec((tm, XG_W), lambda i, pos: (i, 0)), pl.BlockSpec(memory_space=pl.ANY)],
            out_specs=pl.BlockSpec(memory_space=pl.ANY),
            scratch_shapes=[pltpu.SemaphoreType.DMA(())],
        ),
        out_shape=jax.ShapeDtypeStruct((n_pad, XG_W), F32),
        input_output_aliases={2: 0},
        compiler_params=_params(("arbitrary",)),
        name="moe_dispatch",
    )(pos, xg, jnp.zeros((n_pad, XG_W), F32))


def _combine_kernel(pos_ref, sorted_hbm, o_ref, sem, *, tm):
    base = pl.program_id(0) * tm
    _row_copies(tm, lambda r0, u: pltpu.make_async_copy(
        sorted_hbm.at[pl.ds(pos_ref[base + r0 + u], 1), :],
        o_ref.at[pl.ds(r0, SUBLANES), :].at[pl.ds(u, 1), :], sem))


def _combine(pos, y_sorted, n):
    tm = TM_SORT
    return pl.pallas_call(
        functools.partial(_combine_kernel, tm=tm),
        grid_spec=pltpu.PrefetchScalarGridSpec(
            num_scalar_prefetch=1,
            grid=(n // tm,),
            in_specs=[pl.BlockSpec(memory_space=pl.ANY)],
            out_specs=pl.BlockSpec((tm, D_MODEL), lambda i, pos: (i, 0)),
            scratch_shapes=[pltpu.SemaphoreType.DMA(())],
        ),
        out_shape=jax.ShapeDtypeStruct((n, D_MODEL), F32),
        compiler_params=_params(("arbitrary",)),
        name="moe_combine",
    )(pos, y_sorted)


def _moe_sorted_kernel(tg_ref, nv_ref, xg_ref, g2_ref, wg_ref, wu_ref, wd_ref, o_ref):
    j = pl.program_id(0)

    @pl.when(j >= nv_ref[0])
    def _():
        o_ref[...] = jnp.zeros(o_ref.shape, F32)

    @pl.when(j < nv_ref[0])
    def _():
        first_expert = tg_ref[j] * E_PER
        x1 = xg_ref[:, :D_MODEL]
        gates = xg_ref[:, D_MODEL:]
        hb = (_rms(x1) * g2_ref[...]).astype(BF16)
        lane = lax.broadcasted_iota(jnp.int32, gates.shape, 1)
        acc = x1
        for e in range(E_PER):
            ge = jnp.sum(jnp.where(lane == first_expert + e, gates, 0.0), axis=-1, keepdims=True)
            a = jax.nn.silu(_dot(hb, wg_ref[e])) * _dot(hb, wu_ref[e]) * ge
            acc = acc + _dot(a.astype(BF16), wd_ref[e])
        o_ref[...] = acc


def _moe_sorted(tile_group, n_valid, xg_sorted, g2, wg, wu, wd, layer):
    n_pad = xg_sorted.shape[0]
    tm = TM_SORT
    by_group = lambda w: w.reshape(DEPTH, N_GROUPS, E_PER, *w.shape[2:])
    last = lambda j, nv: jnp.minimum(j, nv[0] - 1)
    w_spec = lambda a, b: pl.BlockSpec((None, None, E_PER, a, b), lambda j, tg, nv: (layer, tg[last(j, nv)], 0, 0, 0))
    return pl.pallas_call(
        _moe_sorted_kernel,
        grid_spec=pltpu.PrefetchScalarGridSpec(
            num_scalar_prefetch=2,
            grid=(n_pad // tm,),
            in_specs=[
                pl.BlockSpec((tm, XG_W), lambda j, tg, nv: (last(j, nv), 0)),
                pl.BlockSpec((1, D_MODEL), lambda j, tg, nv: (0, 0)),
                w_spec(D_MODEL, D_E), w_spec(D_MODEL, D_E), w_spec(D_E, D_MODEL),
            ],
            out_specs=pl.BlockSpec((tm, D_MODEL), lambda j, tg, nv: (j, 0)),
        ),
        out_shape=jax.ShapeDtypeStruct((n_pad, D_MODEL), F32),
        compiler_params=_params(("arbitrary",)),
        name="moe_sorted",
    )(tile_group, n_valid, xg_sorted, g2, by_group(wg), by_group(wu), by_group(wd))


def _moe_grouped(xg, counts, g2, wg, wu, wd, layer):
    n = xg.shape[0]
    n_pad = n + N_GROUPS * TM_SORT
    n_tiles = n_pad // TM_SORT
    meta = xg[:, D_MODEL + META_GROUP_LANE:D_MODEL + META_RANK_LANE + 1].astype(jnp.int32)
    tiles_per_group = (counts[:N_GROUPS, 0].astype(jnp.int32) + TM_SORT - 1) // TM_SORT
    tile_end = jnp.cumsum(tiles_per_group)
    pos = ((tile_end - tiles_per_group) * TM_SORT)[meta[:, 0]] + meta[:, 1]
    tile_group = jnp.minimum(jnp.sum(jnp.arange(n_tiles)[:, None] >= tile_end[None, :], axis=1), N_GROUPS - 1)
    xg_sorted = _dispatch(pos, xg, n_pad)
    y_sorted = _moe_sorted(tile_group.astype(jnp.int32), tile_end[N_GROUPS - 1:], xg_sorted, g2, wg, wu, wd, layer)
    return _combine(pos, y_sorted, n)


def _router_weights(r_g, r_e):
    r = jnp.concatenate([r_e, r_g, jnp.zeros((D_MODEL, ROUTER_LANES - N_EXP - N_GROUPS), F32)], axis=1)
    hi = r.astype(BF16)
    lo = (r - hi.astype(F32)).astype(BF16)
    return hi, lo


def kernel(x_prompt, x_sample, cache_k, cache_v, page_table, norm1_g, w_in, a_vnorm_g, a_ws, a_bs, qn_g, kn_g, lam_q1, lam_k1, lam_q2, lam_k2, b_onorm_g, w_out, norm2_g, router_g, router_e, w_gate, w_up, w_down):
    batch, seq, _ = x_prompt.shape
    dec_batch, n_new, _ = x_sample.shape
    xp = x_prompt.reshape(batch * seq, D_MODEL)
    xs = x_sample.reshape(dec_batch * n_new, D_MODEL)

    ii = jnp.arange(MXU_DIM) // DK
    bd = (ii[:, None] == ii[None, :]).astype(BF16)
    eye = jnp.eye(CHUNK // n_new, dtype=F32)

    w_in_b, w_out_b = w_in.astype(BF16), w_out.astype(BF16)
    wg, wu, wd = w_gate.astype(BF16), w_up.astype(BF16), w_down.astype(BF16)

    rows_p, rows_s = None, None
    for l in range(DEPTH):
        lam_init = 0.8 - 0.6 * math.exp(-0.3 * l)
        rhi, rlo = _router_weights(router_g[l], router_e[l])
        g1 = norm1_g[l][None]
        g2 = norm2_g[l][None]
        avg = a_vnorm_g[l][None]
        qg = jnp.tile(qn_g[l], D_QK // DK)[None]
        kg = jnp.tile(kn_g[l], D_QK // DK)[None]
        og = b_onorm_g[l][None]
        lams = (lam_q1[l][None], lam_k1[l][None], lam_q2[l][None], lam_k2[l][None])
        ws_p = a_ws[l]
        bias_p = jnp.repeat(a_bs[l].T, DH_A, axis=1)
        ws_s = jax.vmap(lambda w: jnp.kron(eye, w[:n_new, :n_new]))(a_ws[l])
        bias_s = jnp.tile(jnp.repeat(a_bs[l][:, :n_new].T, DH_A, axis=1), (CHUNK // n_new, 1))

        aout, q, k_all, kb, v_all, vb = _in_proj(xp, g1, w_in_b, avg, qg, kg, ws_p, bias_p, bd, l, rows_p, False)
        rows_p = (k_all, v_all)
        bout = _attn_prompt(q, kb, vb, *lams, og, lam_init, batch, seq)
        xg, counts = _merge_router(aout, bout, xp, w_out_b, l, g2, rhi, rlo, True)
        xp = _moe_grouped(xg, counts, g2, wg, wu, wd, l)

        aout, q, k_all, kb, v_all, vb, va_all = _in_proj(xs, g1, w_in_b, avg, qg, kg, ws_s, bias_s, bd, l, rows_s, True)
        rows_s = (k_all, v_all, va_all)
        bout = _attn_sample(q, kb, vb, cache_k, cache_v, page_table, l, *lams, og, lam_init, dec_batch, n_new)
        xg, h = _merge_router(aout, bout, xs, w_out_b, l, g2, rhi, rlo, False)
        xs = _moe(h, xg, wg, wu, wd, l)

    return (xp.reshape(batch, seq, D_MODEL), xs.reshape(dec_batch, n_new, D_MODEL),
            rows_p[0].reshape(DEPTH, batch, seq, H_B, 2 * DK), rows_p[1].reshape(DEPTH, batch, seq, H_B, DV),
            rows_s[0].reshape(DEPTH, dec_batch, n_new, H_B, 2 * DK), rows_s[1].reshape(DEPTH, dec_batch, n_new, H_B, DV),
            rows_s[2].reshape(DEPTH, dec_batch, n_new, D_A))
```

```python
import functools
import math

import jax
import jax.numpy as jnp
from jax import lax
from jax.experimental import pallas as pl
from jax.experimental.pallas import tpu as pltpu

F32 = jnp.float32
BF16 = jnp.bfloat16

D_MODEL = 1024
DEPTH = 2
D_A = 512
H_A = 8
DH_A = D_A // H_A
CHUNK = 128
D_B = 512
H_B = 4
DV = D_B // H_B
DK = DV // 2
D_QK = H_B * 2 * DK
D_IN = 2 * D_A + 2 * D_QK + D_B
ATTN_SCALE = DK ** -0.5
LOG2E = math.log2(math.e)
DVA = DV + 16
NEG_INF = -1e30
N_GROUPS = 4
E_PER = 4
N_EXP = N_GROUPS * E_PER
D_E = 256
EPS = 1e-6

LANES = 128
MXU_DIM = 256
VMEM_LIMIT_BYTES = 48 * 1024 * 1024

TM_PROJ = 512
TQ = 512
TK = 512
assert TQ == TK
TM_MOE = 1024
ROUTER_LANES = 128
GROUP_LANE0 = N_EXP
XG_W = D_MODEL + ROUTER_LANES
META_GROUP_LANE = N_EXP
META_RANK_LANE = N_EXP + 1
TM_SORT = 512
ROUTER_ROWS = 24


def _params(sem):
    return pltpu.CompilerParams(dimension_semantics=sem, vmem_limit_bytes=VMEM_LIMIT_BYTES)


def _const_spec(shape):
    nd = len(shape)
    return pl.BlockSpec(shape, lambda *_: (0,) * nd)


def _rms(x):
    return x * lax.rsqrt(jnp.mean(x * x, axis=-1, keepdims=True) + EPS)


def _dot(a, b):
    return jnp.dot(a, b, preferred_element_type=F32)


def _dot_nt(a, b):
    return lax.dot_general(a, b, (((1,), (1,)), ((), ())), preferred_element_type=F32)


N_PROJ_IN = 9


def _in_proj_kernel(*refs, tm, n_alias, emit_va):
    x_ref, g1_ref, w_ref, avg_ref, qg_ref, kg_ref, ws_ref, bias_ref, bd_ref = refs[:N_PROJ_IN]
    outs = refs[N_PROJ_IN + n_alias:]
    aout_ref, q_ref, k32_ref, kb_ref, v32_ref, vb_ref = outs[:6]
    x = x_ref[...]
    xb = (_rms(x) * g1_ref[...]).astype(BF16)

    a = jax.nn.gelu(_dot(xb, w_ref[:, 0:2 * D_A]))
    u = a[:, :D_A]
    va = _rms(a[:, D_A:]) * avg_ref[...]
    if emit_va:
        outs[6][...] = va
    vab = va.astype(BF16)

    row = lax.broadcasted_iota(jnp.int32, (CHUNK, CHUNK), 0)
    col = lax.broadcasted_iota(jnp.int32, (CHUNK, CHUNK), 1)
    tril = row >= col
    wsm = [jnp.where(tril, ws_ref[h], 0.0).astype(BF16) for h in range(H_A)]
    first_head = lax.broadcasted_iota(jnp.int32, (CHUNK, LANES), 1) < DH_A
    for c in range(tm // CHUNK):
        rows = slice(c * CHUNK, (c + 1) * CHUNK)
        for hp in range(H_A // 2):
            cols = slice(hp * LANES, (hp + 1) * LANES)
            blk = vab[rows, cols]
            s = jnp.where(first_head, _dot(wsm[2 * hp], blk), _dot(wsm[2 * hp + 1], blk))
            s = s + bias_ref[:, cols]
            aout_ref[rows, cols] = (u[rows, cols] * s).astype(BF16)

    bd = bd_ref[...]

    def group_norm(z, g):
        sq = (z * z).astype(BF16)
        ss = jnp.concatenate([_dot(sq[:, i * MXU_DIM:(i + 1) * MXU_DIM], bd) for i in range(D_QK // MXU_DIM)], axis=1)
        return z * lax.rsqrt(ss * (1.0 / DK) + EPS) * g

    qn = group_norm(_dot(xb, w_ref[:, 2 * D_A:2 * D_A + D_QK]), qg_ref[...])
    q_ref[...] = (qn * (ATTN_SCALE * LOG2E)).astype(BF16)
    kn = group_norm(_dot(xb, w_ref[:, 2 * D_A + D_QK:2 * D_A + 2 * D_QK]), kg_ref[...])
    kb_ref[...] = kn.astype(BF16)
    v = _dot(xb, w_ref[:, 2 * D_A + 2 * D_QK:D_IN])
    vb_ref[...] = v.astype(BF16)
    for h in range(H_B):
        k32_ref[pl.ds(h, tm, stride=H_B), :] = kn[:, h * LANES:(h + 1) * LANES]
        v32_ref[pl.ds(h, tm, stride=H_B), :] = v[:, h * LANES:(h + 1) * LANES]


def _in_proj(x2d, g1, w_in_b, avg, qg, kg, ws, bias, bd, layer, prev, emit_va):
    n = x2d.shape[0]
    tm = TM_PROJ
    row_spec = lambda w: pl.BlockSpec((tm, w), lambda i: (i, 0))
    rows4 = jax.ShapeDtypeStruct((DEPTH, n * H_B, LANES), F32)
    rows4_spec = pl.BlockSpec((None, tm * H_B, LANES), lambda i: (layer, i, 0))
    out_shapes = [
        jax.ShapeDtypeStruct((n, D_A), BF16),
        jax.ShapeDtypeStruct((n, D_QK), BF16),
        rows4,
        jax.ShapeDtypeStruct((n, D_QK), BF16),
        rows4,
        jax.ShapeDtypeStruct((n, D_B), BF16),
    ]
    out_specs = [row_spec(D_A), row_spec(D_QK), rows4_spec, row_spec(D_QK), rows4_spec, row_spec(D_B)]
    if emit_va:
        out_shapes.append(jax.ShapeDtypeStruct((DEPTH, n, D_A), F32))
        out_specs.append(pl.BlockSpec((None, tm, D_A), lambda i: (layer, i, 0)))
    prev = () if prev is None else tuple(prev)
    alias_out = (2, 4, 6)[:len(prev)]
    return pl.pallas_call(
        functools.partial(_in_proj_kernel, tm=tm, n_alias=len(prev), emit_va=emit_va),
        grid=(n // tm,),
        in_specs=[
            row_spec(D_MODEL),
            _const_spec((1, D_MODEL)),
            pl.BlockSpec((None, D_MODEL, D_IN), lambda i: (layer, 0, 0)),
            _const_spec((1, D_A)),
            _const_spec((1, D_QK)),
            _const_spec((1, D_QK)),
            _const_spec((H_A, CHUNK, CHUNK)),
            _const_spec((CHUNK, D_A)),
            _const_spec((MXU_DIM, MXU_DIM)),
        ] + [pl.BlockSpec(memory_space=pl.ANY)] * len(prev),
        out_specs=tuple(out_specs),
        out_shape=tuple(out_shapes),
        input_output_aliases={N_PROJ_IN + j: o for j, o in enumerate(alias_out)},
        compiler_params=_params(("parallel",)),
        name="in_proj",
    )(x2d, g1, w_in_b, avg, qg, kg, ws, bias, bd, *prev)


def _lam(lq1_ref, lk1_ref, lq2_ref, lk2_ref, lam_init):
    s1 = jnp.sum(lq1_ref[...] * lk1_ref[...], axis=-1, keepdims=True)
    s2 = jnp.sum(lq2_ref[...] * lk2_ref[...], axis=-1, keepdims=True)
    return jnp.exp(s1) - jnp.exp(s2) + lam_init


def _out_norm(o, og, lam_init):
    return _rms(o) * og * (1.0 - lam_init)


def _attn_prompt_kernel(q_ref, k_ref, v_ref, lq1_ref, lk1_ref, lq2_ref, lk2_ref, og_ref, o_ref,
                        vt_ref, sa_ref, sb_ref, m_ref, acc_ref, *, lam_init, seq):
    ones = jnp.ones((DVA - DV, TK), BF16)
    for j in range(seq // TK):
        vt_ref[j, 0:DV, :] = v_ref[j * TK:(j + 1) * TK, :].astype(F32).T.astype(BF16)
        vt_ref[j, DV:DVA, :] = ones
    lam = _lam(lq1_ref, lk1_ref, lq2_ref, lk2_ref, lam_init)
    lane = lax.broadcasted_iota(jnp.int32, (TQ, LANES), 1)

    for qi in range(seq // TQ):
        q = q_ref[qi * TQ:(qi + 1) * TQ, :]
        zero = jnp.zeros_like(q)
        qs = jnp.concatenate([jnp.where(lane < DK, q, zero), jnp.where(lane >= DK, q, zero)], axis=0)

        m_ref[...] = jnp.full(m_ref.shape, NEG_INF, F32)
        acc_ref[...] = jnp.zeros(acc_ref.shape, F32)

        def scores(kj, st_ref, qs=qs):
            start = kj * TK if isinstance(kj, int) else pl.multiple_of(kj * TK, TK)
            st_ref[...] = _dot_nt(k_ref[pl.ds(start, TK), :], qs)

        def step(kj, st_ref, diagonal):
            if diagonal:
                r = lax.broadcasted_iota(jnp.int32, (TK, 2 * TQ), 0)
                c = lax.broadcasted_iota(jnp.int32, (TK, 2 * TQ), 1)
                st_ref[...] = jnp.where(r <= jnp.where(c >= TQ, c - TQ, c), st_ref[...], NEG_INF)
            m_prev = m_ref[...]
            m_new = jnp.maximum(m_prev, jnp.max(st_ref[...], axis=0, keepdims=True))
            alpha = jnp.exp2(m_prev - m_new)
            p = jnp.exp2(st_ref[...] - m_new).astype(BF16)
            acc_ref[...] = alpha * acc_ref[...] + _dot(vt_ref[kj], p)
            m_ref[...] = m_new

        scores(0, sa_ref)

        def pair(i, carry, scores=scores, step=step):
            kj = 2 * i
            scores(kj + 1, sb_ref)
            step(kj, sa_ref, False)
            scores(kj + 2, sa_ref)
            step(kj + 1, sb_ref, False)
            return carry

        if qi // 2 > 0:
            lax.fori_loop(0, qi // 2, pair, 0)
        if qi % 2 == 1:
            scores(qi, sb_ref)
            step(qi - 1, sa_ref, False)
            step(qi, sb_ref, True)
        else:
            step(qi, sa_ref, True)

        acc = acc_ref[...]
        o = acc[0:DV] * (1.0 / acc[DV:DV + 1])
        ot = o[:, :TQ] - lam * o[:, TQ:]
        o_ref[qi * TQ:(qi + 1) * TQ, :] = _out_norm(ot.T, og_ref[...], lam_init).astype(BF16)


def _attn_prompt(q, kb, vb, lq1, lk1, lq2, lk2, og, lam_init, batch, seq):
    q3 = q.reshape(batch, seq, D_QK)
    k3 = kb.reshape(batch, seq, D_QK)
    v3 = vb.reshape(batch, seq, D_B)
    vec = _const_spec((1, DK))
    head_spec = pl.BlockSpec((None, seq, LANES), lambda b, h: (b, 0, h))
    out = pl.pallas_call(
        functools.partial(_attn_prompt_kernel, lam_init=lam_init, seq=seq),
        grid=(batch, H_B),
        in_specs=[head_spec, head_spec, head_spec, vec, vec, vec, vec, _const_spec((1, DV))],
        out_specs=head_spec,
        out_shape=jax.ShapeDtypeStruct((batch, seq, D_B), BF16),
        scratch_shapes=[
            pltpu.VMEM((seq // TK, DVA, TK), BF16),
            pltpu.VMEM((TK, 2 * TQ), F32),
            pltpu.VMEM((TK, 2 * TQ), F32),
            pltpu.VMEM((1, 2 * TQ), F32),
            pltpu.VMEM((DVA, 2 * TQ), F32),
        ],
        compiler_params=_params(("parallel", "parallel")),
        name="attn_prompt",
    )(q3, k3, v3, lq1, lk1, lq2, lk2, og)
    return out.reshape(batch * seq, D_B)


def _attn_sample_kernel(pt_ref, q_ref, kn_ref, vn_ref, lq1_ref, lk1_ref, lq2_ref, lk2_ref, og_ref, *rest,
                        n_pages, n_new, lam_init):
    k_pages = rest[:n_pages]
    v_pages = rest[n_pages:2 * n_pages]
    o_ref, s_ref = rest[2 * n_pages:]
    n_rows = H_B * 2 * n_new
    n_cols = k_pages[0].shape[0]

    q = q_ref[...].astype(F32)
    lane = lax.broadcasted_iota(jnp.int32, (n_new, LANES), 1)
    parts = []
    for h in range(H_B):
        qh = q[:, h * LANES:(h + 1) * LANES]
        parts.append(jnp.where(lane < DK, qh, 0.0))
        parts.append(jnp.where(lane >= DK, qh, 0.0))
    qs = jnp.concatenate(parts, axis=0).astype(BF16)

    def valid_mask(cols, causal):
        r = lax.broadcasted_iota(jnp.int32, (n_rows, cols), 0)
        c = lax.broadcasted_iota(jnp.int32, (n_rows, cols), 1)
        valid = (c & (H_B - 1)) == lax.shift_right_logical(r, int(math.log2(2 * n_new)))
        if causal:
            valid = valid & (lax.shift_right_logical(c, int(math.log2(H_B))) <= (r & (n_new - 1)))
        return valid

    valid = valid_mask(n_cols, False)
    mx = jnp.full((n_rows, n_cols), NEG_INF, F32)
    for p in range(n_pages):
        s = jnp.where(valid, _dot_nt(qs, k_pages[p][...].astype(BF16)), NEG_INF)
        s_ref[:, p * n_cols:(p + 1) * n_cols] = s
        mx = jnp.maximum(mx, s)
    s_new = jnp.where(valid_mask(n_new * H_B, True), _dot_nt(qs, kn_ref[...]), NEG_INF)
    m = jnp.maximum(jnp.max(mx, axis=-1, keepdims=True), jnp.max(s_new, axis=-1, keepdims=True))

    p_new = jnp.exp2(s_new - m)
    acc = _dot(p_new.astype(BF16), vn_ref[...])
    lsum = jnp.zeros((n_rows, n_cols), F32)
    for p in range(n_pages):
        pp = jnp.exp2(s_ref[:, p * n_cols:(p + 1) * n_cols] - m)
        lsum = lsum + pp
        acc = acc + _dot(pp.astype(BF16), v_pages[p][...].astype(BF16))
    l = jnp.sum(lsum, axis=-1, keepdims=True) + jnp.sum(p_new, axis=-1, keepdims=True)

    lam = _lam(lq1_ref, lk1_ref, lq2_ref, lk2_ref, lam_init)
    o = acc / l
    for h in range(H_B):
        base = h * 2 * n_new
        oh = o[base:base + n_new] - lam * o[base + n_new:base + 2 * n_new]
        o_ref[:, h * DV:(h + 1) * DV] = _out_norm(oh, og_ref[...], lam_init)


def _attn_sample(q, kb, vb, cache_k, cache_v, page_table, layer, lq1, lk1, lq2, lk2, og, lam_init, dec_batch, n_new):
    n_pages = page_table.shape[1]
    page = cache_k.shape[2]
    n_pool = cache_k.shape[1]
    ck = cache_k.reshape(DEPTH, n_pool, page * H_B, 2 * DK)
    cv = cache_v.reshape(DEPTH, n_pool, page * H_B, DV)
    q3 = q.reshape(dec_batch, n_new, D_QK)
    kn = kb.reshape(dec_batch, n_new * H_B, 2 * DK)
    vn = vb.reshape(dec_batch, n_new * H_B, DV)
    n_rows = H_B * 2 * n_new
    vec = pl.BlockSpec((1, DK), lambda b, pt: (0, 0))

    def page_spec(p):
        return pl.BlockSpec((None, None, page * H_B, LANES), lambda b, pt: (layer, pt[b * n_pages + p], 0, 0))

    grid_spec = pltpu.PrefetchScalarGridSpec(
        num_scalar_prefetch=1,
        grid=(dec_batch,),
        in_specs=[
            pl.BlockSpec((None, n_new, D_QK), lambda b, pt: (b, 0, 0)),
            pl.BlockSpec((None, n_new * H_B, LANES), lambda b, pt: (b, 0, 0)),
            pl.BlockSpec((None, n_new * H_B, LANES), lambda b, pt: (b, 0, 0)),
            vec, vec, vec, vec,
            pl.BlockSpec((1, DV), lambda b, pt: (0, 0)),
        ] + [page_spec(p) for p in range(n_pages)] * 2,
        out_specs=pl.BlockSpec((None, n_new, D_B), lambda b, pt: (b, 0, 0)),
        scratch_shapes=[pltpu.VMEM((n_rows, n_pages * page * H_B), F32)],
    )
    out = pl.pallas_call(
        functools.partial(_attn_sample_kernel, n_pages=n_pages, n_new=n_new, lam_init=lam_init),
        grid_spec=grid_spec,
        out_shape=jax.ShapeDtypeStruct((dec_batch, n_new, D_B), F32),
        compiler_params=_params(("arbitrary",)),
        name="attn_sample",
    )(page_table.reshape(-1), q3, kn, vn, lq1, lk1, lq2, lk2, og, *([ck] * n_pages), *([cv] * n_pages))
    return out.reshape(dec_batch * n_new, D_B)


def _merge_router_kernel(*refs, grouped):
    a_ref, b_ref, x_ref, wo_ref, g2_ref, rhi_ref, rlo_ref = refs[:7]
    y = _dot(a_ref[...].astype(BF16), wo_ref[0:D_A, :]) + _dot(b_ref[...].astype(BF16), wo_ref[D_A:, :])
    x1 = x_ref[...] + y
    h = _rms(x1) * g2_ref[...]
    hb = h.astype(BF16)
    hlo = (h - hb.astype(F32)).astype(BF16)
    rhi = rhi_ref[...]
    logits = _dot(hb, rhi) + _dot(hlo, rhi) + _dot(hb, rlo_ref[...])

    lt = logits.T[0:ROUTER_ROWS, :]
    row_i = lax.broadcasted_iota(jnp.int32, lt.shape, 0)
    row = row_i.astype(F32)
    row_group = lax.shift_right_logical(row_i, int(math.log2(E_PER))).astype(F32)
    big = float(ROUTER_ROWS)

    def first_row(mask):
        return jnp.min(jnp.where(mask, row, big), axis=0, keepdims=True)

    is_g = (row_i >= GROUP_LANE0) & (row_i < GROUP_LANE0 + N_GROUPS)
    lg = jnp.where(is_g, lt, NEG_INF)
    eg = jnp.where(is_g, jnp.exp(lg - jnp.max(lg, axis=0, keepdims=True)), 0.0)
    pg = eg / jnp.sum(eg, axis=0, keepdims=True)
    gw = jnp.max(pg, axis=0, keepdims=True)
    gi = first_row(is_g & (pg == gw)) - GROUP_LANE0
    sel = (row_i < N_EXP) & (row_group == gi)
    le = jnp.where(sel, lt, NEG_INF)
    ee = jnp.where(sel, jnp.exp(le - jnp.max(le, axis=0, keepdims=True)), 0.0)
    pe = ee / jnp.sum(ee, axis=0, keepdims=True)
    t1 = jnp.max(jnp.where(sel, pe, -1.0), axis=0, keepdims=True)
    i1 = first_row(sel & (pe == t1))
    rest = sel & (row != i1)
    t2 = jnp.max(jnp.where(rest, pe, -1.0), axis=0, keepdims=True)
    i2 = first_row(rest & (pe == t2))
    den = t1 + t2
    gates_t = jnp.where(row == i1, gw * (t1 / den), jnp.where(row == i2, gw * (t2 / den), 0.0))

    if not grouped:
        xg_ref, h_ref = refs[7:]
        h_ref[...] = hb
    else:
        utri_ref, xg_ref, counts_ref, cnt_ref = refs[7:]

        @pl.when(pl.program_id(0) == 0)
        def _():
            cnt_ref[...] = jnp.zeros(cnt_ref.shape, F32)

        row8 = lax.broadcasted_iota(jnp.int32, (SUBLANES, lt.shape[1]), 0).astype(F32)
        onehot = jnp.where(row8 == gi, 1.0, 0.0)
        earlier = _dot(onehot.astype(BF16), utri_ref[...])
        cnt = cnt_ref[...]
        rank = jnp.sum(onehot * (earlier + cnt), axis=0, keepdims=True)
        cnt = cnt + jnp.sum(onehot, axis=1, keepdims=True)
        cnt_ref[...] = cnt
        counts_ref[...] = cnt[:, :ROUTER_LANES]
        gates_t = gates_t + jnp.where(row == META_GROUP_LANE, gi, jnp.where(row == META_RANK_LANE, rank, 0.0))
    pad = jnp.zeros((ROUTER_LANES - ROUTER_ROWS, gates_t.shape[1]), F32)
    xg_ref[:, :D_MODEL] = x1
    xg_ref[:, D_MODEL:] = jnp.concatenate([gates_t, pad], axis=0).T


def _merge_router(aout, bout, x2d, w_out_b, layer, g2, rhi, rlo, grouped):
    n = x2d.shape[0]
    tm = TM_PROJ
    row_spec = lambda w: pl.BlockSpec((tm, w), lambda i: (i, 0))
    in_specs = [
        row_spec(D_A), row_spec(D_B), row_spec(D_MODEL),
        pl.BlockSpec((None, D_MODEL, D_MODEL), lambda i: (layer, 0, 0)),
        _const_spec((1, D_MODEL)),
        _const_spec((D_MODEL, ROUTER_LANES)),
        _const_spec((D_MODEL, ROUTER_LANES)),
    ]
    args = [aout, bout, x2d, w_out_b, g2, rhi, rlo]
    xg_shape = jax.ShapeDtypeStruct((n, XG_W), F32)
    if grouped:
        r = jnp.arange(tm)
        args.append((r[:, None] < r[None, :]).astype(BF16))
        in_specs.append(_const_spec((tm, tm)))
        out_specs = (row_spec(XG_W), _const_spec((8, ROUTER_LANES)))
        out_shape = (xg_shape, jax.ShapeDtypeStruct((8, ROUTER_LANES), F32))
        scratch = [pltpu.VMEM((8, tm), F32)]
    else:
        out_specs = (row_spec(XG_W), row_spec(D_MODEL))
        out_shape = (xg_shape, jax.ShapeDtypeStruct((n, D_MODEL), BF16))
        scratch = []
    return pl.pallas_call(
        functools.partial(_merge_router_kernel, grouped=grouped),
        grid=(n // tm,),
        in_specs=in_specs,
        out_specs=out_specs,
        out_shape=out_shape,
        scratch_shapes=scratch,
        compiler_params=_params(("arbitrary",)),
        name="merge_router",
    )(*args)


def _moe_kernel(h_ref, gates_ref, x1_ref, wg_ref, wu_ref, wd_ref, o_ref):
    e = pl.program_id(1)

    @pl.when(e == 0)
    def _():
        o_ref[...] = x1_ref[...]

    h = h_ref[...]
    gates = gates_ref[...]
    lane = lax.broadcasted_iota(jnp.int32, gates.shape, 1)
    ge = jnp.sum(jnp.where(lane == e, gates, 0.0), axis=-1, keepdims=True)
    a = jax.nn.silu(_dot(h, wg_ref[...])) * _dot(h, wu_ref[...]) * ge
    o_ref[...] += _dot(a.astype(BF16), wd_ref[...])


def _moe(h, xg, wg, wu, wd, layer):
    n = h.shape[0]
    tm = min(TM_MOE, n)
    return pl.pallas_call(
        _moe_kernel,
        grid=(n // tm, N_EXP),
        in_specs=[
            pl.BlockSpec((tm, D_MODEL), lambda i, e: (i, 0)),
            pl.BlockSpec((tm, ROUTER_LANES), lambda i, e: (i, D_MODEL // ROUTER_LANES)),
            pl.BlockSpec((tm, D_MODEL), lambda i, e: (i, 0)),
            pl.BlockSpec((None, None, D_MODEL, D_E), lambda i, e: (layer, e, 0, 0)),
            pl.BlockSpec((None, None, D_MODEL, D_E), lambda i, e: (layer, e, 0, 0)),
            pl.BlockSpec((None, None, D_E, D_MODEL), lambda i, e: (layer, e, 0, 0)),
        ],
        out_specs=pl.BlockSpec((tm, D_MODEL), lambda i, e: (i, 0)),
        out_shape=jax.ShapeDtypeStruct((n, D_MODEL), F32),
        compiler_params=_params(("parallel", "arbitrary")),
        name="moe",
    )(h, xg, xg, wg, wu, wd)


SUBLANES = 8


def _row_copies(n_rows, make_copy):
    def issue(k, carry):
        r0 = pl.multiple_of(k * SUBLANES, SUBLANES)
        for u in range(SUBLANES):
            make_copy(r0, u).start()
        return carry

    def drain(k, carry):
        r0 = pl.multiple_of(k * SUBLANES, SUBLANES)
        for u in range(SUBLANES):
            make_copy(r0, u).wait()
        return carry

    lax.fori_loop(0, n_rows // SUBLANES, issue, 0)
    lax.fori_loop(0, n_rows // SUBLANES, drain, 0)


def _dispatch_kernel(pos_ref, tile_end_ref, xg_ref, sorted_hbm, zero_ref, sem, zero_sem, *, tm):
    @pl.when(pl.program_id(0) == 0)
    def _():
        zero_ref[...] = jnp.zeros(zero_ref.shape, F32)
        for g in range(N_GROUPS):
            first = tile_end_ref[g - 1] if g > 0 else 0

            @pl.when(tile_end_ref[g] > first)
            def _():
                start = pl.multiple_of((tile_end_ref[g] - 1) * tm, tm)
                fill = pltpu.make_async_copy(zero_ref, sorted_hbm.at[pl.ds(start, tm), :], zero_sem)
                fill.start()
                fill.wait()

    base = pl.program_id(0) * tm
    _row_copies(tm, lambda r0, u: pltpu.make_async_copy(
        xg_ref.at[pl.ds(r0, SUBLANES), :].at[pl.ds(u, 1), :],
        sorted_hbm.at[pl.ds(pos_ref[base + r0 + u], 1), :], sem))


def _dispatch(pos, tile_end, xg, n_pad):
    n = xg.shape[0]
    tm = TM_SORT
    return pl.pallas_call(
        functools.partial(_dispatch_kernel, tm=tm),
        grid_spec=pltpu.PrefetchScalarGridSpec(
            num_scalar_prefetch=2,
            grid=(n // tm,),
            in_specs=[pl.BlockSpec((tm, XG_W), lambda i, pos, te: (i, 0))],
            out_specs=pl.BlockSpec(memory_space=pl.ANY),
            scratch_shapes=[pltpu.VMEM((tm, XG_W), F32), pltpu.SemaphoreType.DMA(()), pltpu.SemaphoreType.DMA(())],
        ),
        out_shape=jax.ShapeDtypeStruct((n_pad, XG_W), F32),
        compiler_params=_params(("arbitrary",)),
        name="moe_dispatch",
    )(pos, tile_end, xg)


def _combine_kernel(pos_ref, sorted_hbm, o_ref, sem, *, tm):
    base = pl.program_id(0) * tm
    _row_copies(tm, lambda r0, u: pltpu.make_async_copy(
        sorted_hbm.at[pl.ds(pos_ref[base + r0 + u], 1), :],
        o_ref.at[pl.ds(r0, SUBLANES), :].at[pl.ds(u, 1), :], sem))


def _combine(pos, y_sorted, n):
    tm = TM_SORT
    return pl.pallas_call(
        functools.partial(_combine_kernel, tm=tm),
        grid_spec=pltpu.PrefetchScalarGridSpec(
            num_scalar_prefetch=1,
            grid=(n // tm,),
            in_specs=[pl.BlockSpec(memory_space=pl.ANY)],
            out_specs=pl.BlockSpec((tm, D_MODEL), lambda i, pos: (i, 0)),
            scratch_shapes=[pltpu.SemaphoreType.DMA(())],
        ),
        out_shape=jax.ShapeDtypeStruct((n, D_MODEL), F32),
        compiler_params=_params(("arbitrary",)),
        name="moe_combine",
    )(pos, y_sorted)


def _moe_sorted_kernel(tg_ref, nv_ref, xg_ref, g2_ref, wg_ref, wu_ref, wd_ref, o_ref):
    j = pl.program_id(0)

    @pl.when(j >= nv_ref[0])
    def _():
        o_ref[...] = jnp.zeros(o_ref.shape, F32)

    @pl.when(j < nv_ref[0])
    def _():
        first_expert = tg_ref[j] * E_PER
        x1 = xg_ref[:, :D_MODEL]
        gates = xg_ref[:, D_MODEL:]
        hb = (_rms(x1) * g2_ref[...]).astype(BF16)
        lane = lax.broadcasted_iota(jnp.int32, gates.shape, 1)
        acc = x1
        for e in range(E_PER):
            ge = jnp.sum(jnp.where(lane == first_expert + e, gates, 0.0), axis=-1, keepdims=True)
            a = jax.nn.silu(_dot(hb, wg_ref[e])) * _dot(hb, wu_ref[e]) * ge
            acc = acc + _dot(a.astype(BF16), wd_ref[e])
        o_ref[...] = acc


def _moe_sorted(tile_group, n_valid, xg_sorted, g2, wg, wu, wd, layer):
    n_pad = xg_sorted.shape[0]
    tm = TM_SORT
    by_group = lambda w: w.reshape(DEPTH, N_GROUPS, E_PER, *w.shape[2:])
    last = lambda j, nv: jnp.minimum(j, nv[0] - 1)
    w_spec = lambda a, b: pl.BlockSpec((None, None, E_PER, a, b), lambda j, tg, nv: (layer, tg[last(j, nv)], 0, 0, 0))
    return pl.pallas_call(
        _moe_sorted_kernel,
        grid_spec=pltpu.PrefetchScalarGridSpec(
            num_scalar_prefetch=2,
            grid=(n_pad // tm,),
            in_specs=[
                pl.BlockSpec((tm, XG_W), lambda j, tg, nv: (last(j, nv), 0)),
                pl.BlockSpec((1, D_MODEL), lambda j, tg, nv: (0, 0)),
                w_spec(D_MODEL, D_E), w_spec(D_MODEL, D_E), w_spec(D_E, D_MODEL),
            ],
            out_specs=pl.BlockSpec((tm, D_MODEL), lambda j, tg, nv: (j, 0)),
        ),
        out_shape=jax.ShapeDtypeStruct((n_pad, D_MODEL), F32),
        compiler_params=_params(("arbitrary",)),
        name="moe_sorted",
    )(tile_group, n_valid, xg_sorted, g2, by_group(wg), by_group(wu), by_group(wd))


def _moe_grouped(xg, counts, g2, wg, wu, wd, layer):
    n = xg.shape[0]
    n_pad = n + N_GROUPS * TM_SORT
    n_tiles = n_pad // TM_SORT
    meta = xg[:, D_MODEL + META_GROUP_LANE:D_MODEL + META_RANK_LANE + 1].astype(jnp.int32)
    tiles_per_group = (counts[:N_GROUPS, 0].astype(jnp.int32) + TM_SORT - 1) // TM_SORT
    tile_end = jnp.cumsum(tiles_per_group)
    pos = ((tile_end - tiles_per_group) * TM_SORT)[meta[:, 0]] + meta[:, 1]
    tile_group = jnp.minimum(jnp.sum(jnp.arange(n_tiles)[:, None] >= tile_end[None, :], axis=1), N_GROUPS - 1)
    xg_sorted = _dispatch(pos, tile_end, xg, n_pad)
    y_sorted = _moe_sorted(tile_group.astype(jnp.int32), tile_end[N_GROUPS - 1:], xg_sorted, g2, wg, wu, wd, layer)
    return _combine(pos, y_sorted, n)


def _router_weights(r_g, r_e):
    r = jnp.concatenate([r_e, r_g, jnp.zeros((D_MODEL, ROUTER_LANES - N_EXP - N_GROUPS), F32)], axis=1)
    hi = r.astype(BF16)
    lo = (r - hi.astype(F32)).astype(BF16)
    return hi, lo


def kernel(x_prompt, x_sample, cache_k, cache_v, page_table, norm1_g, w_in, a_vnorm_g, a_ws, a_bs, qn_g, kn_g, lam_q1, lam_k1, lam_q2, lam_k2, b_onorm_g, w_out, norm2_g, router_g, router_e, w_gate, w_up, w_down):
    batch, seq, _ = x_prompt.shape
    dec_batch, n_new, _ = x_sample.shape
    xp = x_prompt.reshape(batch * seq, D_MODEL)
    xs = x_sample.reshape(dec_batch * n_new, D_MODEL)

    ii = jnp.arange(MXU_DIM) // DK
    bd = (ii[:, None] == ii[None, :]).astype(BF16)
    eye = jnp.eye(CHUNK // n_new, dtype=F32)

    w_in_b, w_out_b = w_in.astype(BF16), w_out.astype(BF16)
    wg, wu, wd = w_gate.astype(BF16), w_up.astype(BF16), w_down.astype(BF16)

    rows_p, rows_s = None, None
    for l in range(DEPTH):
        lam_init = 0.8 - 0.6 * math.exp(-0.3 * l)
        rhi, rlo = _router_weights(router_g[l], router_e[l])
        g1 = norm1_g[l][None]
        g2 = norm2_g[l][None]
        avg = a_vnorm_g[l][None]
        qg = jnp.tile(qn_g[l], D_QK // DK)[None]
        kg = jnp.tile(kn_g[l], D_QK // DK)[None]
        og = b_onorm_g[l][None]
        lams = (lam_q1[l][None], lam_k1[l][None], lam_q2[l][None], lam_k2[l][None])
        ws_p = a_ws[l]
        bias_p = jnp.repeat(a_bs[l].T, DH_A, axis=1)
        ws_s = jax.vmap(lambda w: jnp.kron(eye, w[:n_new, :n_new]))(a_ws[l])
        bias_s = jnp.tile(jnp.repeat(a_bs[l][:, :n_new].T, DH_A, axis=1), (CHUNK // n_new, 1))

        aout, q, k_all, kb, v_all, vb = _in_proj(xp, g1, w_in_b, avg, qg, kg, ws_p, bias_p, bd, l, rows_p, False)
        rows_p = (k_all, v_all)
        bout = _attn_prompt(q, kb, vb, *lams, og, lam_init, batch, seq)
        xg, counts = _merge_router(aout, bout, xp, w_out_b, l, g2, rhi, rlo, True)
        xp = _moe_grouped(xg, counts, g2, wg, wu, wd, l)

        aout, q, k_all, kb, v_all, vb, va_all = _in_proj(xs, g1, w_in_b, avg, qg, kg, ws_s, bias_s, bd, l, rows_s, True)
        rows_s = (k_all, v_all, va_all)
        bout = _attn_sample(q, kb, vb, cache_k, cache_v, page_table, l, *lams, og, lam_init, dec_batch, n_new)
        xg, h = _merge_router(aout, bout, xs, w_out_b, l, g2, rhi, rlo, False)
        xs = _moe(h, xg, wg, wu, wd, l)

    return (xp.reshape(batch, seq, D_MODEL), xs.reshape(dec_batch, n_new, D_MODEL),
            rows_p[0].reshape(DEPTH, batch, seq, H_B, 2 * DK), rows_p[1].reshape(DEPTH, batch, seq, H_B, DV),
            rows_s[0].reshape(DEPTH, dec_batch, n_new, H_B, 2 * DK), rows_s[1].reshape(DEPTH, dec_batch, n_new, H_B, DV),
            rows_s[2].reshape(DEPTH, dec_batch, n_new, D_A))
```

```python
import functools
import math

import jax
import jax.numpy as jnp
from jax import lax
from jax.experimental import pallas as pl
from jax.experimental.pallas import tpu as pltpu

F32 = jnp.float32
BF16 = jnp.bfloat16

D_MODEL = 1024
DEPTH = 2
D_A = 512
H_A = 8
DH_A = D_A // H_A
CHUNK = 128
D_B = 512
H_B = 4
DV = D_B // H_B
DK = DV // 2
D_QK = H_B * 2 * DK
D_IN = 2 * D_A + 2 * D_QK + D_B
ATTN_SCALE = DK ** -0.5
LOG2E = math.log2(math.e)
DVA = DV + 16
NEG_INF = -1e30
N_GROUPS = 4
E_PER = 4
N_EXP = N_GROUPS * E_PER
D_E = 256
EPS = 1e-6

LANES = 128
MXU_DIM = 256
VMEM_LIMIT_BYTES = 48 * 1024 * 1024

TM_PROJ = 512
TQ = 512
TK = 512
assert TQ == TK
TM_MOE = 1024
ROUTER_LANES = 128
GROUP_LANE0 = N_EXP
XG_W = D_MODEL + ROUTER_LANES
META_GROUP_LANE = N_EXP
META_RANK_LANE = N_EXP + 1
TM_SORT = 512
TM_MOVE = 2048
ROUTER_ROWS = 24


def _params(sem):
    return pltpu.CompilerParams(dimension_semantics=sem, vmem_limit_bytes=VMEM_LIMIT_BYTES)


def _const_spec(shape):
    nd = len(shape)
    return pl.BlockSpec(shape, lambda *_: (0,) * nd)


def _rms(x):
    return x * lax.rsqrt(jnp.mean(x * x, axis=-1, keepdims=True) + EPS)


def _dot(a, b):
    return jnp.dot(a, b, preferred_element_type=F32)


def _dot_nt(a, b):
    return lax.dot_general(a, b, (((1,), (1,)), ((), ())), preferred_element_type=F32)


N_PROJ_IN = 9


def _in_proj_kernel(*refs, tm, n_alias, emit_va):
    x_ref, g1_ref, w_ref, avg_ref, qg_ref, kg_ref, ws_ref, bias_ref, bd_ref = refs[:N_PROJ_IN]
    outs = refs[N_PROJ_IN + n_alias:]
    aout_ref, q_ref, k32_ref, kb_ref, v32_ref, vb_ref = outs[:6]
    x = x_ref[...]
    xb = (_rms(x) * g1_ref[...]).astype(BF16)

    a = jax.nn.gelu(_dot(xb, w_ref[:, 0:2 * D_A]))
    u = a[:, :D_A]
    va = _rms(a[:, D_A:]) * avg_ref[...]
    if emit_va:
        outs[6][...] = va
    vab = va.astype(BF16)

    row = lax.broadcasted_iota(jnp.int32, (CHUNK, CHUNK), 0)
    col = lax.broadcasted_iota(jnp.int32, (CHUNK, CHUNK), 1)
    tril = row >= col
    wsm = [jnp.where(tril, ws_ref[h], 0.0).astype(BF16) for h in range(H_A)]
    first_head = lax.broadcasted_iota(jnp.int32, (CHUNK, LANES), 1) < DH_A
    for c in range(tm // CHUNK):
        rows = slice(c * CHUNK, (c + 1) * CHUNK)
        for hp in range(H_A // 2):
            cols = slice(hp * LANES, (hp + 1) * LANES)
            blk = vab[rows, cols]
            s = jnp.where(first_head, _dot(wsm[2 * hp], blk), _dot(wsm[2 * hp + 1], blk))
            s = s + bias_ref[:, cols]
            aout_ref[rows, cols] = (u[rows, cols] * s).astype(BF16)

    bd = bd_ref[...]

    def group_norm(z, g):
        sq = (z * z).astype(BF16)
        ss = jnp.concatenate([_dot(sq[:, i * MXU_DIM:(i + 1) * MXU_DIM], bd) for i in range(D_QK // MXU_DIM)], axis=1)
        return z * lax.rsqrt(ss * (1.0 / DK) + EPS) * g

    qn = group_norm(_dot(xb, w_ref[:, 2 * D_A:2 * D_A + D_QK]), qg_ref[...])
    q_ref[...] = (qn * (ATTN_SCALE * LOG2E)).astype(BF16)
    kn = group_norm(_dot(xb, w_ref[:, 2 * D_A + D_QK:2 * D_A + 2 * D_QK]), kg_ref[...])
    kb_ref[...] = kn.astype(BF16)
    v = _dot(xb, w_ref[:, 2 * D_A + 2 * D_QK:D_IN])
    vb_ref[...] = v.astype(BF16)
    for h in range(H_B):
        k32_ref[pl.ds(h, tm, stride=H_B), :] = kn[:, h * LANES:(h + 1) * LANES]
        v32_ref[pl.ds(h, tm, stride=H_B), :] = v[:, h * LANES:(h + 1) * LANES]


def _in_proj(x2d, g1, w_in_b, avg, qg, kg, ws, bias, bd, layer, prev, emit_va):
    n = x2d.shape[0]
    tm = TM_PROJ
    row_spec = lambda w: pl.BlockSpec((tm, w), lambda i: (i, 0))
    rows4 = jax.ShapeDtypeStruct((DEPTH, n * H_B, LANES), F32)
    rows4_spec = pl.BlockSpec((None, tm * H_B, LANES), lambda i: (layer, i, 0))
    out_shapes = [
        jax.ShapeDtypeStruct((n, D_A), BF16),
        jax.ShapeDtypeStruct((n, D_QK), BF16),
        rows4,
        jax.ShapeDtypeStruct((n, D_QK), BF16),
        rows4,
        jax.ShapeDtypeStruct((n, D_B), BF16),
    ]
    out_specs = [row_spec(D_A), row_spec(D_QK), rows4_spec, row_spec(D_QK), rows4_spec, row_spec(D_B)]
    if emit_va:
        out_shapes.append(jax.ShapeDtypeStruct((DEPTH, n, D_A), F32))
        out_specs.append(pl.BlockSpec((None, tm, D_A), lambda i: (layer, i, 0)))
    prev = () if prev is None else tuple(prev)
    alias_out = (2, 4, 6)[:len(prev)]
    return pl.pallas_call(
        functools.partial(_in_proj_kernel, tm=tm, n_alias=len(prev), emit_va=emit_va),
        grid=(n // tm,),
        in_specs=[
            row_spec(D_MODEL),
            _const_spec((1, D_MODEL)),
            pl.BlockSpec((None, D_MODEL, D_IN), lambda i: (layer, 0, 0)),
            _const_spec((1, D_A)),
            _const_spec((1, D_QK)),
            _const_spec((1, D_QK)),
            _const_spec((H_A, CHUNK, CHUNK)),
            _const_spec((CHUNK, D_A)),
            _const_spec((MXU_DIM, MXU_DIM)),
        ] + [pl.BlockSpec(memory_space=pl.ANY)] * len(prev),
        out_specs=tuple(out_specs),
        out_shape=tuple(out_shapes),
        input_output_aliases={N_PROJ_IN + j: o for j, o in enumerate(alias_out)},
        compiler_params=_params(("parallel",)),
        name="in_proj",
    )(x2d, g1, w_in_b, avg, qg, kg, ws, bias, bd, *prev)


def _lam(lq1_ref, lk1_ref, lq2_ref, lk2_ref, lam_init):
    s1 = jnp.sum(lq1_ref[...] * lk1_ref[...], axis=-1, keepdims=True)
    s2 = jnp.sum(lq2_ref[...] * lk2_ref[...], axis=-1, keepdims=True)
    return jnp.exp(s1) - jnp.exp(s2) + lam_init


def _out_norm(o, og, lam_init):
    return _rms(o) * og * (1.0 - lam_init)


def _attn_prompt_kernel(q_ref, k_ref, v_ref, lq1_ref, lk1_ref, lq2_ref, lk2_ref, og_ref, o_ref,
                        vt_ref, sa_ref, sb_ref, m_ref, acc_ref, *, lam_init, seq):
    ones = jnp.ones((DVA - DV, TK), BF16)
    for j in range(seq // TK):
        vt_ref[j, 0:DV, :] = v_ref[j * TK:(j + 1) * TK, :].astype(F32).T.astype(BF16)
        vt_ref[j, DV:DVA, :] = ones
    lam = _lam(lq1_ref, lk1_ref, lq2_ref, lk2_ref, lam_init)
    lane = lax.broadcasted_iota(jnp.int32, (TQ, LANES), 1)

    for qi in range(seq // TQ):
        q = q_ref[qi * TQ:(qi + 1) * TQ, :]
        zero = jnp.zeros_like(q)
        qs = jnp.concatenate([jnp.where(lane < DK, q, zero), jnp.where(lane >= DK, q, zero)], axis=0)

        m_ref[...] = jnp.full(m_ref.shape, NEG_INF, F32)
        acc_ref[...] = jnp.zeros(acc_ref.shape, F32)

        def scores(kj, st_ref, qs=qs):
            start = kj * TK if isinstance(kj, int) else pl.multiple_of(kj * TK, TK)
            st_ref[...] = _dot_nt(k_ref[pl.ds(start, TK), :], qs)

        def step(kj, st_ref, diagonal):
            if diagonal:
                r = lax.broadcasted_iota(jnp.int32, (TK, 2 * TQ), 0)
                c = lax.broadcasted_iota(jnp.int32, (TK, 2 * TQ), 1)
                st_ref[...] = jnp.where(r <= jnp.where(c >= TQ, c - TQ, c), st_ref[...], NEG_INF)
            m_prev = m_ref[...]
            m_new = jnp.maximum(m_prev, jnp.max(st_ref[...], axis=0, keepdims=True))
            alpha = jnp.exp2(m_prev - m_new)
            p = jnp.exp2(st_ref[...] - m_new).astype(BF16)
            acc_ref[...] = alpha * acc_ref[...] + _dot(vt_ref[kj], p)
            m_ref[...] = m_new

        scores(0, sa_ref)

        def pair(i, carry, scores=scores, step=step):
            kj = 2 * i
            scores(kj + 1, sb_ref)
            step(kj, sa_ref, False)
            scores(kj + 2, sa_ref)
            step(kj + 1, sb_ref, False)
            return carry

        if qi // 2 > 0:
            lax.fori_loop(0, qi // 2, pair, 0)
        if qi % 2 == 1:
            scores(qi, sb_ref)
            step(qi - 1, sa_ref, False)
            step(qi, sb_ref, True)
        else:
            step(qi, sa_ref, True)

        acc = acc_ref[...]
        o = acc[0:DV] * (1.0 / acc[DV:DV + 1])
        ot = o[:, :TQ] - lam * o[:, TQ:]
        o_ref[qi * TQ:(qi + 1) * TQ, :] = _out_norm(ot.T, og_ref[...], lam_init).astype(BF16)


def _attn_prompt(q, kb, vb, lq1, lk1, lq2, lk2, og, lam_init, batch, seq):
    q3 = q.reshape(batch, seq, D_QK)
    k3 = kb.reshape(batch, seq, D_QK)
    v3 = vb.reshape(batch, seq, D_B)
    vec = _const_spec((1, DK))
    head_spec = pl.BlockSpec((None, seq, LANES), lambda b, h: (b, 0, h))
    out = pl.pallas_call(
        functools.partial(_attn_prompt_kernel, lam_init=lam_init, seq=seq),
        grid=(batch, H_B),
        in_specs=[head_spec, head_spec, head_spec, vec, vec, vec, vec, _const_spec((1, DV))],
        out_specs=head_spec,
        out_shape=jax.ShapeDtypeStruct((batch, seq, D_B), BF16),
        scratch_shapes=[
            pltpu.VMEM((seq // TK, DVA, TK), BF16),
            pltpu.VMEM((TK, 2 * TQ), F32),
            pltpu.VMEM((TK, 2 * TQ), F32),
            pltpu.VMEM((1, 2 * TQ), F32),
            pltpu.VMEM((DVA, 2 * TQ), F32),
        ],
        compiler_params=_params(("parallel", "parallel")),
        name="attn_prompt",
    )(q3, k3, v3, lq1, lk1, lq2, lk2, og)
    return out.reshape(batch * seq, D_B)


def _attn_sample_kernel(pt_ref, q_ref, kn_ref, vn_ref, lq1_ref, lk1_ref, lq2_ref, lk2_ref, og_ref, *rest,
                        n_pages, n_new, lam_init):
    k_pages = rest[:n_pages]
    v_pages = rest[n_pages:2 * n_pages]
    o_ref, s_ref = rest[2 * n_pages:]
    n_rows = H_B * 2 * n_new
    n_cols = k_pages[0].shape[0]

    q = q_ref[...].astype(F32)
    lane = lax.broadcasted_iota(jnp.int32, (n_new, LANES), 1)
    parts = []
    for h in range(H_B):
        qh = q[:, h * LANES:(h + 1) * LANES]
        parts.append(jnp.where(lane < DK, qh, 0.0))
        parts.append(jnp.where(lane >= DK, qh, 0.0))
    qs = jnp.concatenate(parts, axis=0).astype(BF16)

    def valid_mask(cols, causal):
        r = lax.broadcasted_iota(jnp.int32, (n_rows, cols), 0)
        c = lax.broadcasted_iota(jnp.int32, (n_rows, cols), 1)
        valid = (c & (H_B - 1)) == lax.shift_right_logical(r, int(math.log2(2 * n_new)))
        if causal:
            valid = valid & (lax.shift_right_logical(c, int(math.log2(H_B))) <= (r & (n_new - 1)))
        return valid

    valid = valid_mask(n_cols, False)
    mx = jnp.full((n_rows, n_cols), NEG_INF, F32)
    for p in range(n_pages):
        s = jnp.where(valid, _dot_nt(qs, k_pages[p][...].astype(BF16)), NEG_INF)
        s_ref[:, p * n_cols:(p + 1) * n_cols] = s
        mx = jnp.maximum(mx, s)
    s_new = jnp.where(valid_mask(n_new * H_B, True), _dot_nt(qs, kn_ref[...]), NEG_INF)
    m = jnp.maximum(jnp.max(mx, axis=-1, keepdims=True), jnp.max(s_new, axis=-1, keepdims=True))

    p_new = jnp.exp2(s_new - m)
    acc = _dot(p_new.astype(BF16), vn_ref[...])
    lsum = jnp.zeros((n_rows, n_cols), F32)
    for p in range(n_pages):
        pp = jnp.exp2(s_ref[:, p * n_cols:(p + 1) * n_cols] - m)
        lsum = lsum + pp
        acc = acc + _dot(pp.astype(BF16), v_pages[p][...].astype(BF16))
    l = jnp.sum(lsum, axis=-1, keepdims=True) + jnp.sum(p_new, axis=-1, keepdims=True)

    lam = _lam(lq1_ref, lk1_ref, lq2_ref, lk2_ref, lam_init)
    o = acc / l
    for h in range(H_B):
        base = h * 2 * n_new
        oh = o[base:base + n_new] - lam * o[base + n_new:base + 2 * n_new]
        o_ref[:, h * DV:(h + 1) * DV] = _out_norm(oh, og_ref[...], lam_init)


def _attn_sample(q, kb, vb, cache_k, cache_v, page_table, layer, lq1, lk1, lq2, lk2, og, lam_init, dec_batch, n_new):
    n_pages = page_table.shape[1]
    page = cache_k.shape[2]
    n_pool = cache_k.shape[1]
    ck = cache_k.reshape(DEPTH, n_pool, page * H_B, 2 * DK)
    cv = cache_v.reshape(DEPTH, n_pool, page * H_B, DV)
    q3 = q.reshape(dec_batch, n_new, D_QK)
    kn = kb.reshape(dec_batch, n_new * H_B, 2 * DK)
    vn = vb.reshape(dec_batch, n_new * H_B, DV)
    n_rows = H_B * 2 * n_new
    vec = pl.BlockSpec((1, DK), lambda b, pt: (0, 0))

    def page_spec(p):
        return pl.BlockSpec((None, None, page * H_B, LANES), lambda b, pt: (layer, pt[b * n_pages + p], 0, 0))

    grid_spec = pltpu.PrefetchScalarGridSpec(
        num_scalar_prefetch=1,
        grid=(dec_batch,),
        in_specs=[
            pl.BlockSpec((None, n_new, D_QK), lambda b, pt: (b, 0, 0)),
            pl.BlockSpec((None, n_new * H_B, LANES), lambda b, pt: (b, 0, 0)),
            pl.BlockSpec((None, n_new * H_B, LANES), lambda b, pt: (b, 0, 0)),
            vec, vec, vec, vec,
            pl.BlockSpec((1, DV), lambda b, pt: (0, 0)),
        ] + [page_spec(p) for p in range(n_pages)] * 2,
        out_specs=pl.BlockSpec((None, n_new, D_B), lambda b, pt: (b, 0, 0)),
        scratch_shapes=[pltpu.VMEM((n_rows, n_pages * page * H_B), F32)],
    )
    out = pl.pallas_call(
        functools.partial(_attn_sample_kernel, n_pages=n_pages, n_new=n_new, lam_init=lam_init),
        grid_spec=grid_spec,
        out_shape=jax.ShapeDtypeStruct((dec_batch, n_new, D_B), F32),
        compiler_params=_params(("arbitrary",)),
        name="attn_sample",
    )(page_table.reshape(-1), q3, kn, vn, lq1, lk1, lq2, lk2, og, *([ck] * n_pages), *([cv] * n_pages))
    return out.reshape(dec_batch * n_new, D_B)


def _merge_router_kernel(*refs, grouped, n_sub):
    if grouped:
        @pl.when(pl.program_id(0) == 0)
        def _():
            refs[-1][...] = jnp.zeros(refs[-1].shape, F32)

    cnt = refs[-1][...] if grouped else None
    for sub in range(n_sub):
        cnt = _merge_router_rows(refs, slice(sub * TM_PROJ, (sub + 1) * TM_PROJ), grouped, cnt)
    if grouped:
        refs[-1][...] = cnt
        refs[-2][...] = cnt[:, :ROUTER_LANES]


def _merge_router_rows(refs, rows, grouped, cnt):
    a_ref, b_ref, x_ref, wo_ref, g2_ref, rhi_ref, rlo_ref = refs[:7]
    y = _dot(a_ref[rows, :].astype(BF16), wo_ref[0:D_A, :]) + _dot(b_ref[rows, :].astype(BF16), wo_ref[D_A:, :])
    x1 = x_ref[rows, :] + y
    h = _rms(x1) * g2_ref[...]
    hb = h.astype(BF16)
    hlo = (h - hb.astype(F32)).astype(BF16)
    rhi = rhi_ref[...]
    logits = _dot(hb, rhi) + _dot(hlo, rhi) + _dot(hb, rlo_ref[...])

    lt = logits.T[0:ROUTER_ROWS, :]
    row_i = lax.broadcasted_iota(jnp.int32, lt.shape, 0)
    row = row_i.astype(F32)
    row_group = lax.shift_right_logical(row_i, int(math.log2(E_PER))).astype(F32)
    big = float(ROUTER_ROWS)

    def first_row(mask):
        return jnp.min(jnp.where(mask, row, big), axis=0, keepdims=True)

    is_g = (row_i >= GROUP_LANE0) & (row_i < GROUP_LANE0 + N_GROUPS)
    lg = jnp.where(is_g, lt, NEG_INF)
    eg = jnp.where(is_g, jnp.exp(lg - jnp.max(lg, axis=0, keepdims=True)), 0.0)
    pg = eg / jnp.sum(eg, axis=0, keepdims=True)
    gw = jnp.max(pg, axis=0, keepdims=True)
    gi = first_row(is_g & (pg == gw)) - GROUP_LANE0
    sel = (row_i < N_EXP) & (row_group == gi)
    le = jnp.where(sel, lt, NEG_INF)
    ee = jnp.where(sel, jnp.exp(le - jnp.max(le, axis=0, keepdims=True)), 0.0)
    pe = ee / jnp.sum(ee, axis=0, keepdims=True)
    t1 = jnp.max(jnp.where(sel, pe, -1.0), axis=0, keepdims=True)
    i1 = first_row(sel & (pe == t1))
    rest = sel & (row != i1)
    t2 = jnp.max(jnp.where(rest, pe, -1.0), axis=0, keepdims=True)
    i2 = first_row(rest & (pe == t2))
    den = t1 + t2
    gates_t = jnp.where(row == i1, gw * (t1 / den), jnp.where(row == i2, gw * (t2 / den), 0.0))

    if not grouped:
        xg_ref, h_ref = refs[7:]
        h_ref[rows, :] = hb
    else:
        utri_ref, xg_ref, counts_ref, cnt_ref = refs[7:]
        row8 = lax.broadcasted_iota(jnp.int32, (SUBLANES, lt.shape[1]), 0).astype(F32)
        onehot = jnp.where(row8 == gi, 1.0, 0.0)
        earlier = _dot(onehot.astype(BF16), utri_ref[...])
        rank = jnp.sum(onehot * (earlier + cnt), axis=0, keepdims=True)
        cnt = cnt + jnp.sum(onehot, axis=1, keepdims=True)
        gates_t = gates_t + jnp.where(row == META_GROUP_LANE, gi, jnp.where(row == META_RANK_LANE, rank, 0.0))
    pad = jnp.zeros((ROUTER_LANES - ROUTER_ROWS, gates_t.shape[1]), F32)
    xg_ref[rows, :D_MODEL] = x1
    xg_ref[rows, D_MODEL:] = jnp.concatenate([gates_t, pad], axis=0).T
    return cnt


def _merge_router(aout, bout, x2d, w_out_b, layer, g2, rhi, rlo, grouped):
    n = x2d.shape[0]
    n_sub = 2
    tm = n_sub * TM_PROJ
    row_spec = lambda w: pl.BlockSpec((tm, w), lambda i: (i, 0))
    in_specs = [
        row_spec(D_A), row_spec(D_B), row_spec(D_MODEL),
        pl.BlockSpec((None, D_MODEL, D_MODEL), lambda i: (layer, 0, 0)),
        _const_spec((1, D_MODEL)),
        _const_spec((D_MODEL, ROUTER_LANES)),
        _const_spec((D_MODEL, ROUTER_LANES)),
    ]
    args = [aout, bout, x2d, w_out_b, g2, rhi, rlo]
    xg_shape = jax.ShapeDtypeStruct((n, XG_W), F32)
    if grouped:
        r = jnp.arange(TM_PROJ)
        args.append((r[:, None] < r[None, :]).astype(BF16))
        in_specs.append(_const_spec((TM_PROJ, TM_PROJ)))
        out_specs = (row_spec(XG_W), _const_spec((8, ROUTER_LANES)))
        out_shape = (xg_shape, jax.ShapeDtypeStruct((8, ROUTER_LANES), F32))
        scratch = [pltpu.VMEM((8, TM_PROJ), F32)]
    else:
        out_specs = (row_spec(XG_W), row_spec(D_MODEL))
        out_shape = (xg_shape, jax.ShapeDtypeStruct((n, D_MODEL), BF16))
        scratch = []
    return pl.pallas_call(
        functools.partial(_merge_router_kernel, grouped=grouped, n_sub=n_sub),
        grid=(n // tm,),
        in_specs=in_specs,
        out_specs=out_specs,
        out_shape=out_shape,
        scratch_shapes=scratch,
        compiler_params=_params(("arbitrary",)),
        name="merge_router",
    )(*args)


def _moe_kernel(h_ref, gates_ref, x1_ref, wg_ref, wu_ref, wd_ref, o_ref):
    e = pl.program_id(1)

    @pl.when(e == 0)
    def _():
        o_ref[...] = x1_ref[...]

    h = h_ref[...]
    gates = gates_ref[...]
    lane = lax.broadcasted_iota(jnp.int32, gates.shape, 1)
    ge = jnp.sum(jnp.where(lane == e, gates, 0.0), axis=-1, keepdims=True)
    a = jax.nn.silu(_dot(h, wg_ref[...])) * _dot(h, wu_ref[...]) * ge
    o_ref[...] += _dot(a.astype(BF16), wd_ref[...])


def _moe(h, xg, wg, wu, wd, layer):
    n = h.shape[0]
    tm = min(TM_MOE, n)
    return pl.pallas_call(
        _moe_kernel,
        grid=(n // tm, N_EXP),
        in_specs=[
            pl.BlockSpec((tm, D_MODEL), lambda i, e: (i, 0)),
            pl.BlockSpec((tm, ROUTER_LANES), lambda i, e: (i, D_MODEL // ROUTER_LANES)),
            pl.BlockSpec((tm, D_MODEL), lambda i, e: (i, 0)),
            pl.BlockSpec((None, None, D_MODEL, D_E), lambda i, e: (layer, e, 0, 0)),
            pl.BlockSpec((None, None, D_MODEL, D_E), lambda i, e: (layer, e, 0, 0)),
            pl.BlockSpec((None, None, D_E, D_MODEL), lambda i, e: (layer, e, 0, 0)),
        ],
        out_specs=pl.BlockSpec((tm, D_MODEL), lambda i, e: (i, 0)),
        out_shape=jax.ShapeDtypeStruct((n, D_MODEL), F32),
        compiler_params=_params(("parallel", "arbitrary")),
        name="moe",
    )(h, xg, xg, wg, wu, wd)


SUBLANES = 8


def _row_copies(n_rows, make_copy):
    def issue(k, carry):
        r0 = pl.multiple_of(k * SUBLANES, SUBLANES)
        for u in range(SUBLANES):
            make_copy(r0, u).start()
        return carry

    def drain(k, carry):
        r0 = pl.multiple_of(k * SUBLANES, SUBLANES)
        for u in range(SUBLANES):
            make_copy(r0, u).wait()
        return carry

    lax.fori_loop(0, n_rows // SUBLANES, issue, 0)
    lax.fori_loop(0, n_rows // SUBLANES, drain, 0)


def _dispatch_kernel(pos_ref, tile_end_ref, xg_ref, sorted_hbm, zero_ref, sem, zero_sem, *, tm):
    @pl.when(pl.program_id(0) == 0)
    def _():
        zero_ref[...] = jnp.zeros(zero_ref.shape, F32)
        for g in range(N_GROUPS):
            first = tile_end_ref[g - 1] if g > 0 else 0

            @pl.when(tile_end_ref[g] > first)
            def _():
                start = pl.multiple_of((tile_end_ref[g] - 1) * TM_SORT, TM_SORT)
                fill = pltpu.make_async_copy(zero_ref, sorted_hbm.at[pl.ds(start, TM_SORT), :], zero_sem)
                fill.start()
                fill.wait()

    base = pl.program_id(0) * tm
    _row_copies(tm, lambda r0, u: pltpu.make_async_copy(
        xg_ref.at[pl.ds(r0, SUBLANES), :].at[pl.ds(u, 1), :],
        sorted_hbm.at[pl.ds(pos_ref[base + r0 + u], 1), :], sem))


def _dispatch(pos, tile_end, xg, n_pad):
    n = xg.shape[0]
    tm = min(TM_MOVE, n)
    return pl.pallas_call(
        functools.partial(_dispatch_kernel, tm=tm),
        grid_spec=pltpu.PrefetchScalarGridSpec(
            num_scalar_prefetch=2,
            grid=(n // tm,),
            in_specs=[pl.BlockSpec((tm, XG_W), lambda i, pos, te: (i, 0))],
            out_specs=pl.BlockSpec(memory_space=pl.ANY),
            scratch_shapes=[pltpu.VMEM((TM_SORT, XG_W), F32), pltpu.SemaphoreType.DMA(()), pltpu.SemaphoreType.DMA(())],
        ),
        out_shape=jax.ShapeDtypeStruct((n_pad, XG_W), F32),
        compiler_params=_params(("arbitrary",)),
        name="moe_dispatch",
    )(pos, tile_end, xg)


def _combine_kernel(pos_ref, sorted_hbm, o_ref, sem, *, tm):
    base = pl.program_id(0) * tm
    _row_copies(tm, lambda r0, u: pltpu.make_async_copy(
        sorted_hbm.at[pl.ds(pos_ref[base + r0 + u], 1), :],
        o_ref.at[pl.ds(r0, SUBLANES), :].at[pl.ds(u, 1), :], sem))


def _combine(pos, y_sorted, n):
    tm = min(TM_MOVE, n)
    return pl.pallas_call(
        functools.partial(_combine_kernel, tm=tm),
        grid_spec=pltpu.PrefetchScalarGridSpec(
            num_scalar_prefetch=1,
            grid=(n // tm,),
            in_specs=[pl.BlockSpec(memory_space=pl.ANY)],
            out_specs=pl.BlockSpec((tm, D_MODEL), lambda i, pos: (i, 0)),
            scratch_shapes=[pltpu.SemaphoreType.DMA(())],
        ),
        out_shape=jax.ShapeDtypeStruct((n, D_MODEL), F32),
        compiler_params=_params(("arbitrary",)),
        name="moe_combine",
    )(pos, y_sorted)


def _moe_sorted_kernel(tg_ref, nv_ref, xg_ref, g2_ref, wg_ref, wu_ref, wd_ref, o_ref):
    j = pl.program_id(0)

    @pl.when(j >= nv_ref[0])
    def _():
        o_ref[...] = jnp.zeros(o_ref.shape, F32)

    @pl.when(j < nv_ref[0])
    def _():
        first_expert = tg_ref[j] * E_PER
        x1 = xg_ref[:, :D_MODEL]
        gates = xg_ref[:, D_MODEL:]
        hb = (_rms(x1) * g2_ref[...]).astype(BF16)
        lane = lax.broadcasted_iota(jnp.int32, gates.shape, 1)
        acc = x1
        for e in range(E_PER):
            ge = jnp.sum(jnp.where(lane == first_expert + e, gates, 0.0), axis=-1, keepdims=True)
            a = jax.nn.silu(_dot(hb, wg_ref[e])) * _dot(hb, wu_ref[e]) * ge
            acc = acc + _dot(a.astype(BF16), wd_ref[e])
        o_ref[...] = acc


def _moe_sorted(tile_group, n_valid, xg_sorted, g2, wg, wu, wd, layer):
    n_pad = xg_sorted.shape[0]
    tm = TM_SORT
    by_group = lambda w: w.reshape(DEPTH, N_GROUPS, E_PER, *w.shape[2:])
    last = lambda j, nv: jnp.minimum(j, nv[0] - 1)
    w_spec = lambda a, b: pl.BlockSpec((None, None, E_PER, a, b), lambda j, tg, nv: (layer, tg[last(j, nv)], 0, 0, 0))
    return pl.pallas_call(
        _moe_sorted_kernel,
        grid_spec=pltpu.PrefetchScalarGridSpec(
            num_scalar_prefetch=2,
            grid=(n_pad // tm,),
            in_specs=[
                pl.BlockSpec((tm, XG_W), lambda j, tg, nv: (last(j, nv), 0)),
                pl.BlockSpec((1, D_MODEL), lambda j, tg, nv: (0, 0)),
                w_spec(D_MODEL, D_E), w_spec(D_MODEL, D_E), w_spec(D_E, D_MODEL),
            ],
            out_specs=pl.BlockSpec((tm, D_MODEL), lambda j, tg, nv: (j, 0)),
        ),
        out_shape=jax.ShapeDtypeStruct((n_pad, D_MODEL), F32),
        compiler_params=_params(("arbitrary",)),
        name="moe_sorted",
    )(tile_group, n_valid, xg_sorted, g2, by_group(wg), by_group(wu), by_group(wd))


def _moe_grouped(xg, counts, g2, wg, wu, wd, layer):
    n = xg.shape[0]
    n_pad = n + N_GROUPS * TM_SORT
    n_tiles = n_pad // TM_SORT
    meta = xg[:, D_MODEL + META_GROUP_LANE:D_MODEL + META_RANK_LANE + 1].astype(jnp.int32)
    tiles_per_group = (counts[:N_GROUPS, 0].astype(jnp.int32) + TM_SORT - 1) // TM_SORT
    tile_end = jnp.cumsum(tiles_per_group)
    pos = ((tile_end - tiles_per_group) * TM_SORT)[meta[:, 0]] + meta[:, 1]
    tile_group = jnp.minimum(jnp.sum(jnp.arange(n_tiles)[:, None] >= tile_end[None, :], axis=1), N_GROUPS - 1)
    xg_sorted = _dispatch(pos, tile_end, xg, n_pad)
    y_sorted = _moe_sorted(tile_group.astype(jnp.int32), tile_end[N_GROUPS - 1:], xg_sorted, g2, wg, wu, wd, layer)
    return _combine(pos, y_sorted, n)


def _router_weights(r_g, r_e):
    r = jnp.concatenate([r_e, r_g, jnp.zeros((D_MODEL, ROUTER_LANES - N_EXP - N_GROUPS), F32)], axis=1)
    hi = r.astype(BF16)
    lo = (r - hi.astype(F32)).astype(BF16)
    return hi, lo


def kernel(x_prompt, x_sample, cache_k, cache_v, page_table, norm1_g, w_in, a_vnorm_g, a_ws, a_bs, qn_g, kn_g, lam_q1, lam_k1, lam_q2, lam_k2, b_onorm_g, w_out, norm2_g, router_g, router_e, w_gate, w_up, w_down):
    batch, seq, _ = x_prompt.shape
    dec_batch, n_new, _ = x_sample.shape
    xp = x_prompt.reshape(batch * seq, D_MODEL)
    xs = x_sample.reshape(dec_batch * n_new, D_MODEL)

    ii = jnp.arange(MXU_DIM) // DK
    bd = (ii[:, None] == ii[None, :]).astype(BF16)
    eye = jnp.eye(CHUNK // n_new, dtype=F32)

    w_in_b, w_out_b = w_in.astype(BF16), w_out.astype(BF16)
    wg, wu, wd = w_gate.astype(BF16), w_up.astype(BF16), w_down.astype(BF16)

    rows_p, rows_s = None, None
    for l in range(DEPTH):
        lam_init = 0.8 - 0.6 * math.exp(-0.3 * l)
        rhi, rlo = _router_weights(router_g[l], router_e[l])
        g1 = norm1_g[l][None]
        g2 = norm2_g[l][None]
        avg = a_vnorm_g[l][None]
        qg = jnp.tile(qn_g[l], D_QK // DK)[None]
        kg = jnp.tile(kn_g[l], D_QK // DK)[None]
        og = b_onorm_g[l][None]
        lams = (lam_q1[l][None], lam_k1[l][None], lam_q2[l][None], lam_k2[l][None])
        ws_p = a_ws[l]
        bias_p = jnp.repeat(a_bs[l].T, DH_A, axis=1)
        ws_s = jax.vmap(lambda w: jnp.kron(eye, w[:n_new, :n_new]))(a_ws[l])
        bias_s = jnp.tile(jnp.repeat(a_bs[l][:, :n_new].T, DH_A, axis=1), (CHUNK // n_new, 1))

        aout, q, k_all, kb, v_all, vb = _in_proj(xp, g1, w_in_b, avg, qg, kg, ws_p, bias_p, bd, l, rows_p, False)
        rows_p = (k_all, v_all)
        bout = _attn_prompt(q, kb, vb, *lams, og, lam_init, batch, seq)
        xg, counts = _merge_router(aout, bout, xp, w_out_b, l, g2, rhi, rlo, True)
        xp = _moe_grouped(xg, counts, g2, wg, wu, wd, l)

        aout, q, k_all, kb, v_all, vb, va_all = _in_proj(xs, g1, w_in_b, avg, qg, kg, ws_s, bias_s, bd, l, rows_s, True)
        rows_s = (k_all, v_all, va_all)
        bout = _attn_sample(q, kb, vb, cache_k, cache_v, page_table, l, *lams, og, lam_init, dec_batch, n_new)
        xg, h = _merge_router(aout, bout, xs, w_out_b, l, g2, rhi, rlo, False)
        xs = _moe(h, xg, wg, wu, wd, l)

    return (xp.reshape(batch, seq, D_MODEL), xs.reshape(dec_batch, n_new, D_MODEL),
            rows_p[0].reshape(DEPTH, batch, seq, H_B, 2 * DK), rows_p[1].reshape(DEPTH, batch, seq, H_B, DV),
            rows_s[0].reshape(DEPTH, dec_batch, n_new, H_B, 2 * DK), rows_s[1].reshape(DEPTH, dec_batch, n_new, H_B, DV),
            rows_s[2].reshape(DEPTH, dec_batch, n_new, D_A))
```

```python
import functools
import math

import jax
import jax.numpy as jnp
from jax import lax
from jax.experimental import pallas as pl
from jax.experimental.pallas import tpu as pltpu

F32 = jnp.float32
BF16 = jnp.bfloat16

D_MODEL = 1024
DEPTH = 2
D_A = 512
H_A = 8
DH_A = D_A // H_A
CHUNK = 128
D_B = 512
H_B = 4
DV = D_B // H_B
DK = DV // 2
D_QK = H_B * 2 * DK
D_IN = 2 * D_A + 2 * D_QK + D_B
ATTN_SCALE = DK ** -0.5
LOG2E = math.log2(math.e)
DVA = DV + 16
NEG_INF = -1e30
N_GROUPS = 4
E_PER = 4
N_EXP = N_GROUPS * E_PER
D_E = 256
EPS = 1e-6

LANES = 128
MXU_DIM = 256
VMEM_LIMIT_BYTES = 48 * 1024 * 1024

TM_PROJ = 512
TQ = 512
TK = 512
assert TQ == TK
TM_MOE = 1024
ROUTER_LANES = 128
GROUP_LANE0 = N_EXP
XG_W = D_MODEL + ROUTER_LANES
META_GROUP_LANE = N_EXP
META_RANK_LANE = N_EXP + 1
TM_SORT = 512
TM_MOVE = 2048
PAGE_SLOTS = 3
ROUTER_ROWS = 24


def _params(sem):
    return pltpu.CompilerParams(dimension_semantics=sem, vmem_limit_bytes=VMEM_LIMIT_BYTES)


def _const_spec(shape):
    nd = len(shape)
    return pl.BlockSpec(shape, lambda *_: (0,) * nd)


def _rms(x):
    return x * lax.rsqrt(jnp.mean(x * x, axis=-1, keepdims=True) + EPS)


def _dot(a, b):
    return jnp.dot(a, b, preferred_element_type=F32)


def _dot_nt(a, b):
    return lax.dot_general(a, b, (((1,), (1,)), ((), ())), preferred_element_type=F32)


N_PROJ_IN = 9


def _in_proj_kernel(*refs, tm, n_alias, emit_va):
    x_ref, g1_ref, w_ref, avg_ref, qg_ref, kg_ref, ws_ref, bias_ref, bd_ref = refs[:N_PROJ_IN]
    outs = refs[N_PROJ_IN + n_alias:]
    aout_ref, q_ref, k32_ref, kb_ref, v32_ref, vb_ref = outs[:6]
    x = x_ref[...]
    xb = (_rms(x) * g1_ref[...]).astype(BF16)

    a = jax.nn.gelu(_dot(xb, w_ref[:, 0:2 * D_A]))
    u = a[:, :D_A]
    va = _rms(a[:, D_A:]) * avg_ref[...]
    if emit_va:
        outs[6][...] = va
    vab = va.astype(BF16)

    row = lax.broadcasted_iota(jnp.int32, (CHUNK, CHUNK), 0)
    col = lax.broadcasted_iota(jnp.int32, (CHUNK, CHUNK), 1)
    tril = row >= col
    wsm = [jnp.where(tril, ws_ref[h], 0.0).astype(BF16) for h in range(H_A)]
    first_head = lax.broadcasted_iota(jnp.int32, (CHUNK, LANES), 1) < DH_A
    for c in range(tm // CHUNK):
        rows = slice(c * CHUNK, (c + 1) * CHUNK)
        for hp in range(H_A // 2):
            cols = slice(hp * LANES, (hp + 1) * LANES)
            blk = vab[rows, cols]
            s = jnp.where(first_head, _dot(wsm[2 * hp], blk), _dot(wsm[2 * hp + 1], blk))
            s = s + bias_ref[:, cols]
            aout_ref[rows, cols] = (u[rows, cols] * s).astype(BF16)

    bd = bd_ref[...]

    def group_norm(z, g):
        sq = (z * z).astype(BF16)
        ss = jnp.concatenate([_dot(sq[:, i * MXU_DIM:(i + 1) * MXU_DIM], bd) for i in range(D_QK // MXU_DIM)], axis=1)
        return z * lax.rsqrt(ss * (1.0 / DK) + EPS) * g

    qn = group_norm(_dot(xb, w_ref[:, 2 * D_A:2 * D_A + D_QK]), qg_ref[...])
    q_ref[...] = (qn * (ATTN_SCALE * LOG2E)).astype(BF16)
    kn = group_norm(_dot(xb, w_ref[:, 2 * D_A + D_QK:2 * D_A + 2 * D_QK]), kg_ref[...])
    kb_ref[...] = kn.astype(BF16)
    v = _dot(xb, w_ref[:, 2 * D_A + 2 * D_QK:D_IN])
    vb_ref[...] = v.astype(BF16)
    for h in range(H_B):
        k32_ref[pl.ds(h, tm, stride=H_B), :] = kn[:, h * LANES:(h + 1) * LANES]
        v32_ref[pl.ds(h, tm, stride=H_B), :] = v[:, h * LANES:(h + 1) * LANES]


def _in_proj(x2d, g1, w_in_b, avg, qg, kg, ws, bias, bd, layer, prev, emit_va):
    n = x2d.shape[0]
    tm = TM_PROJ
    row_spec = lambda w: pl.BlockSpec((tm, w), lambda i: (i, 0))
    rows4 = jax.ShapeDtypeStruct((DEPTH, n * H_B, LANES), F32)
    rows4_spec = pl.BlockSpec((None, tm * H_B, LANES), lambda i: (layer, i, 0))
    out_shapes = [
        jax.ShapeDtypeStruct((n, D_A), BF16),
        jax.ShapeDtypeStruct((n, D_QK), BF16),
        rows4,
        jax.ShapeDtypeStruct((n, D_QK), BF16),
        rows4,
        jax.ShapeDtypeStruct((n, D_B), BF16),
    ]
    out_specs = [row_spec(D_A), row_spec(D_QK), rows4_spec, row_spec(D_QK), rows4_spec, row_spec(D_B)]
    if emit_va:
        out_shapes.append(jax.ShapeDtypeStruct((DEPTH, n, D_A), F32))
        out_specs.append(pl.BlockSpec((None, tm, D_A), lambda i: (layer, i, 0)))
    prev = () if prev is None else tuple(prev)
    alias_out = (2, 4, 6)[:len(prev)]
    return pl.pallas_call(
        functools.partial(_in_proj_kernel, tm=tm, n_alias=len(prev), emit_va=emit_va),
        grid=(n // tm,),
        in_specs=[
            row_spec(D_MODEL),
            _const_spec((1, D_MODEL)),
            pl.BlockSpec((None, D_MODEL, D_IN), lambda i: (layer, 0, 0)),
            _const_spec((1, D_A)),
            _const_spec((1, D_QK)),
            _const_spec((1, D_QK)),
            _const_spec((H_A, CHUNK, CHUNK)),
            _const_spec((CHUNK, D_A)),
            _const_spec((MXU_DIM, MXU_DIM)),
        ] + [pl.BlockSpec(memory_space=pl.ANY)] * len(prev),
        out_specs=tuple(out_specs),
        out_shape=tuple(out_shapes),
        input_output_aliases={N_PROJ_IN + j: o for j, o in enumerate(alias_out)},
        compiler_params=_params(("parallel",)),
        name="in_proj",
    )(x2d, g1, w_in_b, avg, qg, kg, ws, bias, bd, *prev)


def _lam(lq1_ref, lk1_ref, lq2_ref, lk2_ref, lam_init):
    s1 = jnp.sum(lq1_ref[...] * lk1_ref[...], axis=-1, keepdims=True)
    s2 = jnp.sum(lq2_ref[...] * lk2_ref[...], axis=-1, keepdims=True)
    return jnp.exp(s1) - jnp.exp(s2) + lam_init


def _out_norm(o, og, lam_init):
    return _rms(o) * og * (1.0 - lam_init)


def _attn_prompt_kernel(q_ref, k_ref, v_ref, lq1_ref, lk1_ref, lq2_ref, lk2_ref, og_ref, o_ref,
                        vt_ref, sa_ref, sb_ref, m_ref, acc_ref, *, lam_init, seq):
    ones = jnp.ones((DVA - DV, TK), BF16)
    for j in range(seq // TK):
        vt_ref[j, 0:DV, :] = v_ref[j * TK:(j + 1) * TK, :].astype(F32).T.astype(BF16)
        vt_ref[j, DV:DVA, :] = ones
    lam = _lam(lq1_ref, lk1_ref, lq2_ref, lk2_ref, lam_init)
    lane = lax.broadcasted_iota(jnp.int32, (TQ, LANES), 1)

    for qi in range(seq // TQ):
        q = q_ref[qi * TQ:(qi + 1) * TQ, :]
        zero = jnp.zeros_like(q)
        qs = jnp.concatenate([jnp.where(lane < DK, q, zero), jnp.where(lane >= DK, q, zero)], axis=0)

        m_ref[...] = jnp.full(m_ref.shape, NEG_INF, F32)
        acc_ref[...] = jnp.zeros(acc_ref.shape, F32)

        def scores(kj, st_ref, qs=qs):
            start = kj * TK if isinstance(kj, int) else pl.multiple_of(kj * TK, TK)
            st_ref[...] = _dot_nt(k_ref[pl.ds(start, TK), :], qs)

        def step(kj, st_ref, diagonal):
            if diagonal:
                r = lax.broadcasted_iota(jnp.int32, (TK, 2 * TQ), 0)
                c = lax.broadcasted_iota(jnp.int32, (TK, 2 * TQ), 1)
                st_ref[...] = jnp.where(r <= jnp.where(c >= TQ, c - TQ, c), st_ref[...], NEG_INF)
            m_prev = m_ref[...]
            m_new = jnp.maximum(m_prev, jnp.max(st_ref[...], axis=0, keepdims=True))
            alpha = jnp.exp2(m_prev - m_new)
            p = jnp.exp2(st_ref[...] - m_new).astype(BF16)
            acc_ref[...] = alpha * acc_ref[...] + _dot(vt_ref[kj], p)
            m_ref[...] = m_new

        scores(0, sa_ref)

        def pair(i, carry, scores=scores, step=step):
            kj = 2 * i
            scores(kj + 1, sb_ref)
            step(kj, sa_ref, False)
            scores(kj + 2, sa_ref)
            step(kj + 1, sb_ref, False)
            return carry

        if qi // 2 > 0:
            lax.fori_loop(0, qi // 2, pair, 0)
        if qi % 2 == 1:
            scores(qi, sb_ref)
            step(qi - 1, sa_ref, False)
            step(qi, sb_ref, True)
        else:
            step(qi, sa_ref, True)

        acc = acc_ref[...]
        o = acc[0:DV] * (1.0 / acc[DV:DV + 1])
        ot = o[:, :TQ] - lam * o[:, TQ:]
        o_ref[qi * TQ:(qi + 1) * TQ, :] = _out_norm(ot.T, og_ref[...], lam_init).astype(BF16)


def _attn_prompt(q, kb, vb, lq1, lk1, lq2, lk2, og, lam_init, batch, seq):
    q3 = q.reshape(batch, seq, D_QK)
    k3 = kb.reshape(batch, seq, D_QK)
    v3 = vb.reshape(batch, seq, D_B)
    vec = _const_spec((1, DK))
    head_spec = pl.BlockSpec((None, seq, LANES), lambda b, h: (b, 0, h))
    out = pl.pallas_call(
        functools.partial(_attn_prompt_kernel, lam_init=lam_init, seq=seq),
        grid=(batch, H_B),
        in_specs=[head_spec, head_spec, head_spec, vec, vec, vec, vec, _const_spec((1, DV))],
        out_specs=head_spec,
        out_shape=jax.ShapeDtypeStruct((batch, seq, D_B), BF16),
        scratch_shapes=[
            pltpu.VMEM((seq // TK, DVA, TK), BF16),
            pltpu.VMEM((TK, 2 * TQ), F32),
            pltpu.VMEM((TK, 2 * TQ), F32),
            pltpu.VMEM((1, 2 * TQ), F32),
            pltpu.VMEM((DVA, 2 * TQ), F32),
        ],
        compiler_params=_params(("parallel", "parallel")),
        name="attn_prompt",
    )(q3, k3, v3, lq1, lk1, lq2, lk2, og)
    return out.reshape(batch * seq, D_B)


def _attn_sample_kernel(pt_ref, q_ref, kn_ref, vn_ref, lq1_ref, lk1_ref, lq2_ref, lk2_ref, og_ref, ck_hbm, cv_hbm,
                        o_ref, kbuf_ref, vbuf_ref, s_ref, sems, *, layer, n_pages, n_new, lam_init):
    b = pl.program_id(0)
    n_b = pl.num_programs(0)

    def pages(bb, slot):
        copies = []
        for p in range(n_pages):
            page_id = pt_ref[bb * n_pages + p]
            copies.append(pltpu.make_async_copy(ck_hbm.at[layer, page_id], kbuf_ref.at[slot, p], sems.at[slot]))
            copies.append(pltpu.make_async_copy(cv_hbm.at[layer, page_id], vbuf_ref.at[slot, p], sems.at[slot]))
        return copies

    @pl.when(b == 0)
    def _():
        for ahead in range(PAGE_SLOTS - 1):
            for cp in pages(ahead, ahead):
                cp.start()

    nxt = b + (PAGE_SLOTS - 1)

    @pl.when(nxt < n_b)
    def _():
        for cp in pages(nxt, lax.rem(nxt, PAGE_SLOTS)):
            cp.start()

    slot = lax.rem(b, PAGE_SLOTS)
    for cp in pages(b, slot):
        cp.wait()
    k_pages = [kbuf_ref.at[slot, p] for p in range(n_pages)]
    v_pages = [vbuf_ref.at[slot, p] for p in range(n_pages)]
    n_rows = H_B * 2 * n_new
    n_cols = kbuf_ref.shape[2]

    q = q_ref[...].astype(F32)
    lane = lax.broadcasted_iota(jnp.int32, (n_new, LANES), 1)
    parts = []
    for h in range(H_B):
        qh = q[:, h * LANES:(h + 1) * LANES]
        parts.append(jnp.where(lane < DK, qh, 0.0))
        parts.append(jnp.where(lane >= DK, qh, 0.0))
    qs = jnp.concatenate(parts, axis=0).astype(BF16)

    def valid_mask(cols, causal):
        r = lax.broadcasted_iota(jnp.int32, (n_rows, cols), 0)
        c = lax.broadcasted_iota(jnp.int32, (n_rows, cols), 1)
        valid = (c & (H_B - 1)) == lax.shift_right_logical(r, int(math.log2(2 * n_new)))
        if causal:
            valid = valid & (lax.shift_right_logical(c, int(math.log2(H_B))) <= (r & (n_new - 1)))
        return valid

    valid = valid_mask(n_cols, False)
    mx = jnp.full((n_rows, n_cols), NEG_INF, F32)
    for p in range(n_pages):
        s = jnp.where(valid, _dot_nt(qs, k_pages[p][...].astype(BF16)), NEG_INF)
        s_ref[:, p * n_cols:(p + 1) * n_cols] = s
        mx = jnp.maximum(mx, s)
    s_new = jnp.where(valid_mask(n_new * H_B, True), _dot_nt(qs, kn_ref[...]), NEG_INF)
    m = jnp.maximum(jnp.max(mx, axis=-1, keepdims=True), jnp.max(s_new, axis=-1, keepdims=True))

    p_new = jnp.exp2(s_new - m)
    acc = _dot(p_new.astype(BF16), vn_ref[...])
    lsum = jnp.zeros((n_rows, n_cols), F32)
    for p in range(n_pages):
        pp = jnp.exp2(s_ref[:, p * n_cols:(p + 1) * n_cols] - m)
        lsum = lsum + pp
        acc = acc + _dot(pp.astype(BF16), v_pages[p][...].astype(BF16))
    l = jnp.sum(lsum, axis=-1, keepdims=True) + jnp.sum(p_new, axis=-1, keepdims=True)

    lam = _lam(lq1_ref, lk1_ref, lq2_ref, lk2_ref, lam_init)
    o = acc / l
    for h in range(H_B):
        base = h * 2 * n_new
        oh = o[base:base + n_new] - lam * o[base + n_new:base + 2 * n_new]
        o_ref[:, h * DV:(h + 1) * DV] = _out_norm(oh, og_ref[...], lam_init)


def _attn_sample(q, kb, vb, cache_k, cache_v, page_table, layer, lq1, lk1, lq2, lk2, og, lam_init, dec_batch, n_new):
    n_pages = page_table.shape[1]
    page = cache_k.shape[2]
    n_pool = cache_k.shape[1]
    ck = cache_k.reshape(DEPTH, n_pool, page * H_B, 2 * DK)
    cv = cache_v.reshape(DEPTH, n_pool, page * H_B, DV)
    q3 = q.reshape(dec_batch, n_new, D_QK)
    kn = kb.reshape(dec_batch, n_new * H_B, 2 * DK)
    vn = vb.reshape(dec_batch, n_new * H_B, DV)
    n_rows = H_B * 2 * n_new
    vec = pl.BlockSpec((1, DK), lambda b, pt: (0, 0))

    any_spec = pl.BlockSpec(memory_space=pl.ANY)
    grid_spec = pltpu.PrefetchScalarGridSpec(
        num_scalar_prefetch=1,
        grid=(dec_batch,),
        in_specs=[
            pl.BlockSpec((None, n_new, D_QK), lambda b, pt: (b, 0, 0)),
            pl.BlockSpec((None, n_new * H_B, LANES), lambda b, pt: (b, 0, 0)),
            pl.BlockSpec((None, n_new * H_B, LANES), lambda b, pt: (b, 0, 0)),
            vec, vec, vec, vec,
            pl.BlockSpec((1, DV), lambda b, pt: (0, 0)),
            any_spec, any_spec,
        ],
        out_specs=pl.BlockSpec((None, n_new, D_B), lambda b, pt: (b, 0, 0)),
        scratch_shapes=[
            pltpu.VMEM((PAGE_SLOTS, n_pages, page * H_B, LANES), F32),
            pltpu.VMEM((PAGE_SLOTS, n_pages, page * H_B, LANES), F32),
            pltpu.VMEM((n_rows, n_pages * page * H_B), F32),
            pltpu.SemaphoreType.DMA((PAGE_SLOTS,)),
        ],
    )
    out = pl.pallas_call(
        functools.partial(_attn_sample_kernel, layer=layer, n_pages=n_pages, n_new=n_new, lam_init=lam_init),
        grid_spec=grid_spec,
        out_shape=jax.ShapeDtypeStruct((dec_batch, n_new, D_B), F32),
        compiler_params=_params(("arbitrary",)),
        name="attn_sample",
    )(page_table.reshape(-1), q3, kn, vn, lq1, lk1, lq2, lk2, og, ck, cv)
    return out.reshape(dec_batch * n_new, D_B)


def _merge_router_kernel(*refs, grouped, n_sub):
    if grouped:
        @pl.when(pl.program_id(0) == 0)
        def _():
            refs[-1][...] = jnp.zeros(refs[-1].shape, F32)

    cnt = refs[-1][...] if grouped else None
    for sub in range(n_sub):
        cnt = _merge_router_rows(refs, slice(sub * TM_PROJ, (sub + 1) * TM_PROJ), grouped, cnt)
    if grouped:
        refs[-1][...] = cnt
        refs[-2][...] = cnt[:, :ROUTER_LANES]


def _merge_router_rows(refs, rows, grouped, cnt):
    a_ref, b_ref, x_ref, wo_ref, g2_ref, rhi_ref, rlo_ref = refs[:7]
    y = _dot(a_ref[rows, :].astype(BF16), wo_ref[0:D_A, :]) + _dot(b_ref[rows, :].astype(BF16), wo_ref[D_A:, :])
    x1 = x_ref[rows, :] + y
    h = _rms(x1) * g2_ref[...]
    hb = h.astype(BF16)
    hlo = (h - hb.astype(F32)).astype(BF16)
    rhi = rhi_ref[...]
    logits = _dot(hb, rhi) + _dot(hlo, rhi) + _dot(hb, rlo_ref[...])

    lt = logits.T[0:ROUTER_ROWS, :]
    row_i = lax.broadcasted_iota(jnp.int32, lt.shape, 0)
    row = row_i.astype(F32)
    row_group = lax.shift_right_logical(row_i, int(math.log2(E_PER))).astype(F32)
    big = float(ROUTER_ROWS)

    def first_row(mask):
        return jnp.min(jnp.where(mask, row, big), axis=0, keepdims=True)

    is_g = (row_i >= GROUP_LANE0) & (row_i < GROUP_LANE0 + N_GROUPS)
    lg = jnp.where(is_g, lt, NEG_INF)
    eg = jnp.where(is_g, jnp.exp(lg - jnp.max(lg, axis=0, keepdims=True)), 0.0)
    pg = eg / jnp.sum(eg, axis=0, keepdims=True)
    gw = jnp.max(pg, axis=0, keepdims=True)
    gi = first_row(is_g & (pg == gw)) - GROUP_LANE0
    sel = (row_i < N_EXP) & (row_group == gi)
    le = jnp.where(sel, lt, NEG_INF)
    ee = jnp.where(sel, jnp.exp(le - jnp.max(le, axis=0, keepdims=True)), 0.0)
    pe = ee / jnp.sum(ee, axis=0, keepdims=True)
    t1 = jnp.max(jnp.where(sel, pe, -1.0), axis=0, keepdims=True)
    i1 = first_row(sel & (pe == t1))
    rest = sel & (row != i1)
    t2 = jnp.max(jnp.where(rest, pe, -1.0), axis=0, keepdims=True)
    i2 = first_row(rest & (pe == t2))
    den = t1 + t2
    gates_t = jnp.where(row == i1, gw * (t1 / den), jnp.where(row == i2, gw * (t2 / den), 0.0))

    if not grouped:
        xg_ref, h_ref = refs[7:]
        h_ref[rows, :] = hb
    else:
        utri_ref, xg_ref, counts_ref, cnt_ref = refs[7:]
        row8 = lax.broadcasted_iota(jnp.int32, (SUBLANES, lt.shape[1]), 0).astype(F32)
        onehot = jnp.where(row8 == gi, 1.0, 0.0)
        earlier = _dot(onehot.astype(BF16), utri_ref[...])
        rank = jnp.sum(onehot * (earlier + cnt), axis=0, keepdims=True)
        cnt = cnt + jnp.sum(onehot, axis=1, keepdims=True)
        gates_t = gates_t + jnp.where(row == META_GROUP_LANE, gi, jnp.where(row == META_RANK_LANE, rank, 0.0))
    pad = jnp.zeros((ROUTER_LANES - ROUTER_ROWS, gates_t.shape[1]), F32)
    xg_ref[rows, :D_MODEL] = x1
    xg_ref[rows, D_MODEL:] = jnp.concatenate([gates_t, pad], axis=0).T
    return cnt


def _merge_router(aout, bout, x2d, w_out_b, layer, g2, rhi, rlo, grouped):
    n = x2d.shape[0]
    n_sub = 2
    tm = n_sub * TM_PROJ
    row_spec = lambda w: pl.BlockSpec((tm, w), lambda i: (i, 0))
    in_specs = [
        row_spec(D_A), row_spec(D_B), row_spec(D_MODEL),
        pl.BlockSpec((None, D_MODEL, D_MODEL), lambda i: (layer, 0, 0)),
        _const_spec((1, D_MODEL)),
        _const_spec((D_MODEL, ROUTER_LANES)),
        _const_spec((D_MODEL, ROUTER_LANES)),
    ]
    args = [aout, bout, x2d, w_out_b, g2, rhi, rlo]
    xg_shape = jax.ShapeDtypeStruct((n, XG_W), F32)
    if grouped:
        r = jnp.arange(TM_PROJ)
        args.append((r[:, None] < r[None, :]).astype(BF16))
        in_specs.append(_const_spec((TM_PROJ, TM_PROJ)))
        out_specs = (row_spec(XG_W), _const_spec((8, ROUTER_LANES)))
        out_shape = (xg_shape, jax.ShapeDtypeStruct((8, ROUTER_LANES), F32))
        scratch = [pltpu.VMEM((8, TM_PROJ), F32)]
    else:
        out_specs = (row_spec(XG_W), row_spec(D_MODEL))
        out_shape = (xg_shape, jax.ShapeDtypeStruct((n, D_MODEL), BF16))
        scratch = []
    return pl.pallas_call(
        functools.partial(_merge_router_kernel, grouped=grouped, n_sub=n_sub),
        grid=(n // tm,),
        in_specs=in_specs,
        out_specs=out_specs,
        out_shape=out_shape,
        scratch_shapes=scratch,
        compiler_params=_params(("arbitrary",)),
        name="merge_router",
    )(*args)


def _moe_kernel(h_ref, gates_ref, x1_ref, wg_ref, wu_ref, wd_ref, o_ref):
    e = pl.program_id(1)

    @pl.when(e == 0)
    def _():
        o_ref[...] = x1_ref[...]

    h = h_ref[...]
    gates = gates_ref[...]
    lane = lax.broadcasted_iota(jnp.int32, gates.shape, 1)
    ge = jnp.sum(jnp.where(lane == e, gates, 0.0), axis=-1, keepdims=True)
    a = jax.nn.silu(_dot(h, wg_ref[...])) * _dot(h, wu_ref[...]) * ge
    o_ref[...] += _dot(a.astype(BF16), wd_ref[...])


def _moe(h, xg, wg, wu, wd, layer):
    n = h.shape[0]
    tm = min(TM_MOE, n)
    return pl.pallas_call(
        _moe_kernel,
        grid=(n // tm, N_EXP),
        in_specs=[
            pl.BlockSpec((tm, D_MODEL), lambda i, e: (i, 0)),
            pl.BlockSpec((tm, ROUTER_LANES), lambda i, e: (i, D_MODEL // ROUTER_LANES)),
            pl.BlockSpec((tm, D_MODEL), lambda i, e: (i, 0)),
            pl.BlockSpec((None, None, D_MODEL, D_E), lambda i, e: (layer, e, 0, 0)),
            pl.BlockSpec((None, None, D_MODEL, D_E), lambda i, e: (layer, e, 0, 0)),
            pl.BlockSpec((None, None, D_E, D_MODEL), lambda i, e: (layer, e, 0, 0)),
        ],
        out_specs=pl.BlockSpec((tm, D_MODEL), lambda i, e: (i, 0)),
        out_shape=jax.ShapeDtypeStruct((n, D_MODEL), F32),
        compiler_params=_params(("parallel", "arbitrary")),
        name="moe",
    )(h, xg, xg, wg, wu, wd)


SUBLANES = 8


def _row_copies(n_rows, make_copy):
    def issue(k, carry):
        r0 = pl.multiple_of(k * SUBLANES, SUBLANES)
        for u in range(SUBLANES):
            make_copy(r0, u).start(priority=u % 2)
        return carry

    def drain(k, carry):
        r0 = pl.multiple_of(k * SUBLANES, SUBLANES)
        for u in range(SUBLANES):
            make_copy(r0, u).wait()
        return carry

    lax.fori_loop(0, n_rows // SUBLANES, issue, 0)
    lax.fori_loop(0, n_rows // SUBLANES, drain, 0)


def _dispatch_kernel(pos_ref, tile_end_ref, xg_ref, sorted_hbm, zero_ref, sem, zero_sem, *, tm):
    @pl.when(pl.program_id(0) == 0)
    def _():
        zero_ref[...] = jnp.zeros(zero_ref.shape, F32)
        for g in range(N_GROUPS):
            first = tile_end_ref[g - 1] if g > 0 else 0

            @pl.when(tile_end_ref[g] > first)
            def _():
                start = pl.multiple_of((tile_end_ref[g] - 1) * TM_SORT, TM_SORT)
                fill = pltpu.make_async_copy(zero_ref, sorted_hbm.at[pl.ds(start, TM_SORT), :], zero_sem)
                fill.start()
                fill.wait()

    base = pl.program_id(0) * tm
    _row_copies(tm, lambda r0, u: pltpu.make_async_copy(
        xg_ref.at[pl.ds(r0, SUBLANES), :].at[pl.ds(u, 1), :],
        sorted_hbm.at[pl.ds(pos_ref[base + r0 + u], 1), :], sem))


def _dispatch(pos, tile_end, xg, n_pad):
    n = xg.shape[0]
    tm = min(TM_MOVE, n)
    return pl.pallas_call(
        functools.partial(_dispatch_kernel, tm=tm),
        grid_spec=pltpu.PrefetchScalarGridSpec(
            num_scalar_prefetch=2,
            grid=(n // tm,),
            in_specs=[pl.BlockSpec((tm, XG_W), lambda i, pos, te: (i, 0))],
            out_specs=pl.BlockSpec(memory_space=pl.ANY),
            scratch_shapes=[pltpu.VMEM((TM_SORT, XG_W), F32), pltpu.SemaphoreType.DMA(()), pltpu.SemaphoreType.DMA(())],
        ),
        out_shape=jax.ShapeDtypeStruct((n_pad, XG_W), F32),
        compiler_params=_params(("arbitrary",)),
        name="moe_dispatch",
    )(pos, tile_end, xg)


def _combine_kernel(pos_ref, sorted_hbm, o_ref, sem, *, tm):
    base = pl.program_id(0) * tm
    _row_copies(tm, lambda r0, u: pltpu.make_async_copy(
        sorted_hbm.at[pl.ds(pos_ref[base + r0 + u], 1), :],
        o_ref.at[pl.ds(r0, SUBLANES), :].at[pl.ds(u, 1), :], sem))


def _combine(pos, y_sorted, n):
    tm = min(TM_MOVE, n)
    return pl.pallas_call(
        functools.partial(_combine_kernel, tm=tm),
        grid_spec=pltpu.PrefetchScalarGridSpec(
            num_scalar_prefetch=1,
            grid=(n // tm,),
            in_specs=[pl.BlockSpec(memory_space=pl.ANY)],
            out_specs=pl.BlockSpec((tm, D_MODEL), lambda i, pos: (i, 0)),
            scratch_shapes=[pltpu.SemaphoreType.DMA(())],
        ),
        out_shape=jax.ShapeDtypeStruct((n, D_MODEL), F32),
        compiler_params=_params(("arbitrary",)),
        name="moe_combine",
    )(pos, y_sorted)


def _moe_sorted_kernel(tg_ref, nv_ref, xg_ref, g2_ref, wg_ref, wu_ref, wd_ref, o_ref):
    j = pl.program_id(0)

    @pl.when(j >= nv_ref[0])
    def _():
        o_ref[...] = jnp.zeros(o_ref.shape, F32)

    @pl.when(j < nv_ref[0])
    def _():
        first_expert = tg_ref[j] * E_PER
        x1 = xg_ref[:, :D_MODEL]
        gates = xg_ref[:, D_MODEL:]
        hb = (_rms(x1) * g2_ref[...]).astype(BF16)
        lane = lax.broadcasted_iota(jnp.int32, gates.shape, 1)
        acc = x1
        for e in range(E_PER):
            ge = jnp.sum(jnp.where(lane == first_expert + e, gates, 0.0), axis=-1, keepdims=True)
            a = jax.nn.silu(_dot(hb, wg_ref[e])) * _dot(hb, wu_ref[e]) * ge
            acc = acc + _dot(a.astype(BF16), wd_ref[e])
        o_ref[...] = acc


def _moe_sorted(tile_group, n_valid, xg_sorted, g2, wg, wu, wd, layer):
    n_pad = xg_sorted.shape[0]
    tm = TM_SORT
    by_group = lambda w: w.reshape(DEPTH, N_GROUPS, E_PER, *w.shape[2:])
    last = lambda j, nv: jnp.minimum(j, nv[0] - 1)
    w_spec = lambda a, b: pl.BlockSpec((None, None, E_PER, a, b), lambda j, tg, nv: (layer, tg[last(j, nv)], 0, 0, 0))
    return pl.pallas_call(
        _moe_sorted_kernel,
        grid_spec=pltpu.PrefetchScalarGridSpec(
            num_scalar_prefetch=2,
            grid=(n_pad // tm,),
            in_specs=[
                pl.BlockSpec((tm, XG_W), lambda j, tg, nv: (last(j, nv), 0)),
                pl.BlockSpec((1, D_MODEL), lambda j, tg, nv: (0, 0)),
                w_spec(D_MODEL, D_E), w_spec(D_MODEL, D_E), w_spec(D_E, D_MODEL),
            ],
            out_specs=pl.BlockSpec((tm, D_MODEL), lambda j, tg, nv: (j, 0)),
        ),
        out_shape=jax.ShapeDtypeStruct((n_pad, D_MODEL), F32),
        compiler_params=_params(("arbitrary",)),
        name="moe_sorted",
    )(tile_group, n_valid, xg_sorted, g2, by_group(wg), by_group(wu), by_group(wd))


def _moe_grouped(xg, counts, g2, wg, wu, wd, layer):
    n = xg.shape[0]
    n_pad = n + N_GROUPS * TM_SORT
    n_tiles = n_pad // TM_SORT
    meta = xg[:, D_MODEL + META_GROUP_LANE:D_MODEL + META_RANK_LANE + 1].astype(jnp.int32)
    tiles_per_group = (counts[:N_GROUPS, 0].astype(jnp.int32) + TM_SORT - 1) // TM_SORT
    tile_end = jnp.cumsum(tiles_per_group)
    pos = ((tile_end - tiles_per_group) * TM_SORT)[meta[:, 0]] + meta[:, 1]
    tile_group = jnp.minimum(jnp.sum(jnp.arange(n_tiles)[:, None] >= tile_end[None, :], axis=1), N_GROUPS - 1)
    xg_sorted = _dispatch(pos, tile_end, xg, n_pad)
    y_sorted = _moe_sorted(tile_group.astype(jnp.int32), tile_end[N_GROUPS - 1:], xg_sorted, g2, wg, wu, wd, layer)
    return _combine(pos, y_sorted, n)


def _router_weights(r_g, r_e):
    r = jnp.concatenate([r_e, r_g, jnp.zeros((D_MODEL, ROUTER_LANES - N_EXP - N_GROUPS), F32)], axis=1)
    hi = r.astype(BF16)
    lo = (r - hi.astype(F32)).astype(BF16)
    return hi, lo


def kernel(x_prompt, x_sample, cache_k, cache_v, page_table, norm1_g, w_in, a_vnorm_g, a_ws, a_bs, qn_g, kn_g, lam_q1, lam_k1, lam_q2, lam_k2, b_onorm_g, w_out, norm2_g, router_g, router_e, w_gate, w_up, w_down):
    batch, seq, _ = x_prompt.shape
    dec_batch, n_new, _ = x_sample.shape
    xp = x_prompt.reshape(batch * seq, D_MODEL)
    xs = x_sample.reshape(dec_batch * n_new, D_MODEL)

    ii = jnp.arange(MXU_DIM) // DK
    bd = (ii[:, None] == ii[None, :]).astype(BF16)
    eye = jnp.eye(CHUNK // n_new, dtype=F32)

    w_in_b, w_out_b = w_in.astype(BF16), w_out.astype(BF16)
    wg, wu, wd = w_gate.astype(BF16), w_up.astype(BF16), w_down.astype(BF16)

    rows_p, rows_s = None, None
    for l in range(DEPTH):
        lam_init = 0.8 - 0.6 * math.exp(-0.3 * l)
        rhi, rlo = _router_weights(router_g[l], router_e[l])
        g1 = norm1_g[l][None]
        g2 = norm2_g[l][None]
        avg = a_vnorm_g[l][None]
        qg = jnp.tile(qn_g[l], D_QK // DK)[None]
        kg = jnp.tile(kn_g[l], D_QK // DK)[None]
        og = b_onorm_g[l][None]
        lams = (lam_q1[l][None], lam_k1[l][None], lam_q2[l][None], lam_k2[l][None])
        ws_p = a_ws[l]
        bias_p = jnp.repeat(a_bs[l].T, DH_A, axis=1)
        ws_s = jax.vmap(lambda w: jnp.kron(eye, w[:n_new, :n_new]))(a_ws[l])
        bias_s = jnp.tile(jnp.repeat(a_bs[l][:, :n_new].T, DH_A, axis=1), (CHUNK // n_new, 1))

        aout, q, k_all, kb, v_all, vb = _in_proj(xp, g1, w_in_b, avg, qg, kg, ws_p, bias_p, bd, l, rows_p, False)
        rows_p = (k_all, v_all)
        bout = _attn_prompt(q, kb, vb, *lams, og, lam_init, batch, seq)
        xg, counts = _merge_router(aout, bout, xp, w_out_b, l, g2, rhi, rlo, True)
        xp = _moe_grouped(xg, counts, g2, wg, wu, wd, l)

        aout, q, k_all, kb, v_all, vb, va_all = _in_proj(xs, g1, w_in_b, avg, qg, kg, ws_s, bias_s, bd, l, rows_s, True)
        rows_s = (k_all, v_all, va_all)
        bout = _attn_sample(q, kb, vb, cache_k, cache_v, page_table, l, *lams, og, lam_init, dec_batch, n_new)
        xg, h = _merge_router(aout, bout, xs, w_out_b, l, g2, rhi, rlo, False)
        xs = _moe(h, xg, wg, wu, wd, l)

    return (xp.reshape(batch, seq, D_MODEL), xs.reshape(dec_batch, n_new, D_MODEL),
            rows_p[0].reshape(DEPTH, batch, seq, H_B, 2 * DK), rows_p[1].reshape(DEPTH, batch, seq, H_B, DV),
            rows_s[0].reshape(DEPTH, dec_batch, n_new, H_B, 2 * DK), rows_s[1].reshape(DEPTH, dec_batch, n_new, H_B, DV),
            rows_s[2].reshape(DEPTH, dec_batch, n_new, D_A))
```

```python
import functools
import math

import jax
import jax.numpy as jnp
from jax import lax
from jax.experimental import pallas as pl
from jax.experimental.pallas import tpu as pltpu

F32 = jnp.float32
BF16 = jnp.bfloat16

D_MODEL = 1024
DEPTH = 2
D_A = 512
H_A = 8
DH_A = D_A // H_A
CHUNK = 128
D_B = 512
H_B = 4
DV = D_B // H_B
DK = DV // 2
D_QK = H_B * 2 * DK
D_IN = 2 * D_A + 2 * D_QK + D_B
ATTN_SCALE = DK ** -0.5
LOG2E = math.log2(math.e)
DVA = DV + 16
NEG_INF = -1e30
N_GROUPS = 4
E_PER = 4
N_EXP = N_GROUPS * E_PER
D_E = 256
EPS = 1e-6

LANES = 128
MXU_DIM = 256
VMEM_LIMIT_BYTES = 48 * 1024 * 1024

TM_PROJ = 512
TQ = 512
TK = 512
assert TQ == TK
TM_MOE = 1024
EXPERTS_PER_STEP = 2
ROUTER_LANES = 128
GROUP_LANE0 = N_EXP
XG_W = D_MODEL + ROUTER_LANES
META_GROUP_LANE = N_EXP
META_RANK_LANE = N_EXP + 1
TM_SORT = 512
TM_MOVE = 2048
PAGE_SLOTS = 3
ROUTER_ROWS = 24


def _params(sem):
    return pltpu.CompilerParams(dimension_semantics=sem, vmem_limit_bytes=VMEM_LIMIT_BYTES)


def _const_spec(shape):
    nd = len(shape)
    return pl.BlockSpec(shape, lambda *_: (0,) * nd)


def _rms(x):
    return x * lax.rsqrt(jnp.mean(x * x, axis=-1, keepdims=True) + EPS)


def _dot(a, b):
    return jnp.dot(a, b, preferred_element_type=F32)


def _dot_nt(a, b):
    return lax.dot_general(a, b, (((1,), (1,)), ((), ())), preferred_element_type=F32)


N_PROJ_IN = 9


def _in_proj_kernel(*refs, tm, n_alias, emit_va):
    x_ref, g1_ref, w_ref, avg_ref, qg_ref, kg_ref, ws_ref, bias_ref, bd_ref = refs[:N_PROJ_IN]
    outs = refs[N_PROJ_IN + n_alias:]
    aout_ref, q_ref, k32_ref, kb_ref, v32_ref, vb_ref = outs[:6]
    x = x_ref[...]
    xb = (_rms(x) * g1_ref[...]).astype(BF16)

    a = jax.nn.gelu(_dot(xb, w_ref[:, 0:2 * D_A]))
    u = a[:, :D_A]
    va = _rms(a[:, D_A:]) * avg_ref[...]
    if emit_va:
        outs[6][...] = va
    vab = va.astype(BF16)

    row = lax.broadcasted_iota(jnp.int32, (CHUNK, CHUNK), 0)
    col = lax.broadcasted_iota(jnp.int32, (CHUNK, CHUNK), 1)
    tril = row >= col
    wsm = [jnp.where(tril, ws_ref[h], 0.0).astype(BF16) for h in range(H_A)]
    first_head = lax.broadcasted_iota(jnp.int32, (CHUNK, LANES), 1) < DH_A
    for c in range(tm // CHUNK):
        rows = slice(c * CHUNK, (c + 1) * CHUNK)
        for hp in range(H_A // 2):
            cols = slice(hp * LANES, (hp + 1) * LANES)
            blk = vab[rows, cols]
            s = jnp.where(first_head, _dot(wsm[2 * hp], blk), _dot(wsm[2 * hp + 1], blk))
            s = s + bias_ref[:, cols]
            aout_ref[rows, cols] = (u[rows, cols] * s).astype(BF16)

    bd = bd_ref[...]

    def group_norm(z, g):
        sq = (z * z).astype(BF16)
        ss = jnp.concatenate([_dot(sq[:, i * MXU_DIM:(i + 1) * MXU_DIM], bd) for i in range(D_QK // MXU_DIM)], axis=1)
        return z * lax.rsqrt(ss * (1.0 / DK) + EPS) * g

    qn = group_norm(_dot(xb, w_ref[:, 2 * D_A:2 * D_A + D_QK]), qg_ref[...])
    q_ref[...] = (qn * (ATTN_SCALE * LOG2E)).astype(BF16)
    kn = group_norm(_dot(xb, w_ref[:, 2 * D_A + D_QK:2 * D_A + 2 * D_QK]), kg_ref[...])
    kb_ref[...] = kn.astype(BF16)
    v = _dot(xb, w_ref[:, 2 * D_A + 2 * D_QK:D_IN])
    vb_ref[...] = v.astype(BF16)
    for h in range(H_B):
        k32_ref[pl.ds(h, tm, stride=H_B), :] = kn[:, h * LANES:(h + 1) * LANES]
        v32_ref[pl.ds(h, tm, stride=H_B), :] = v[:, h * LANES:(h + 1) * LANES]


def _in_proj(x2d, g1, w_in_b, avg, qg, kg, ws, bias, bd, layer, prev, emit_va):
    n = x2d.shape[0]
    tm = TM_PROJ
    row_spec = lambda w: pl.BlockSpec((tm, w), lambda i: (i, 0))
    rows4 = jax.ShapeDtypeStruct((DEPTH, n * H_B, LANES), F32)
    rows4_spec = pl.BlockSpec((None, tm * H_B, LANES), lambda i: (layer, i, 0))
    out_shapes = [
        jax.ShapeDtypeStruct((n, D_A), BF16),
        jax.ShapeDtypeStruct((n, D_QK), BF16),
        rows4,
        jax.ShapeDtypeStruct((n, D_QK), BF16),
        rows4,
        jax.ShapeDtypeStruct((n, D_B), BF16),
    ]
    out_specs = [row_spec(D_A), row_spec(D_QK), rows4_spec, row_spec(D_QK), rows4_spec, row_spec(D_B)]
    if emit_va:
        out_shapes.append(jax.ShapeDtypeStruct((DEPTH, n, D_A), F32))
        out_specs.append(pl.BlockSpec((None, tm, D_A), lambda i: (layer, i, 0)))
    prev = () if prev is None else tuple(prev)
    alias_out = (2, 4, 6)[:len(prev)]
    return pl.pallas_call(
        functools.partial(_in_proj_kernel, tm=tm, n_alias=len(prev), emit_va=emit_va),
        grid=(n // tm,),
        in_specs=[
            row_spec(D_MODEL),
            _const_spec((1, D_MODEL)),
            pl.BlockSpec((None, D_MODEL, D_IN), lambda i: (layer, 0, 0)),
            _const_spec((1, D_A)),
            _const_spec((1, D_QK)),
            _const_spec((1, D_QK)),
            _const_spec((H_A, CHUNK, CHUNK)),
            _const_spec((CHUNK, D_A)),
            _const_spec((MXU_DIM, MXU_DIM)),
        ] + [pl.BlockSpec(memory_space=pl.ANY)] * len(prev),
        out_specs=tuple(out_specs),
        out_shape=tuple(out_shapes),
        input_output_aliases={N_PROJ_IN + j: o for j, o in enumerate(alias_out)},
        compiler_params=_params(("parallel",)),
        name="in_proj",
    )(x2d, g1, w_in_b, avg, qg, kg, ws, bias, bd, *prev)


def _lam(lq1_ref, lk1_ref, lq2_ref, lk2_ref, lam_init):
    s1 = jnp.sum(lq1_ref[...] * lk1_ref[...], axis=-1, keepdims=True)
    s2 = jnp.sum(lq2_ref[...] * lk2_ref[...], axis=-1, keepdims=True)
    return jnp.exp(s1) - jnp.exp(s2) + lam_init


def _out_norm(o, og, lam_init):
    return _rms(o) * og * (1.0 - lam_init)


def _attn_prompt_kernel(q_ref, k_ref, v_ref, lq1_ref, lk1_ref, lq2_ref, lk2_ref, og_ref, o_ref,
                        vt_ref, sa_ref, sb_ref, m_ref, acc_ref, *, lam_init, seq):
    ones = jnp.ones((DVA - DV, TK), BF16)
    for j in range(seq // TK):
        vt_ref[j, 0:DV, :] = v_ref[j * TK:(j + 1) * TK, :].astype(F32).T.astype(BF16)
        vt_ref[j, DV:DVA, :] = ones
    lam = _lam(lq1_ref, lk1_ref, lq2_ref, lk2_ref, lam_init)
    lane = lax.broadcasted_iota(jnp.int32, (TQ, LANES), 1)

    for qi in range(seq // TQ):
        q = q_ref[qi * TQ:(qi + 1) * TQ, :]
        zero = jnp.zeros_like(q)
        qs = jnp.concatenate([jnp.where(lane < DK, q, zero), jnp.where(lane >= DK, q, zero)], axis=0)

        m_ref[...] = jnp.full(m_ref.shape, NEG_INF, F32)
        acc_ref[...] = jnp.zeros(acc_ref.shape, F32)

        def scores(kj, st_ref, qs=qs):
            start = kj * TK if isinstance(kj, int) else pl.multiple_of(kj * TK, TK)
            st_ref[...] = _dot_nt(k_ref[pl.ds(start, TK), :], qs)

        def step(kj, st_ref, diagonal):
            if diagonal:
                r = lax.broadcasted_iota(jnp.int32, (TK, 2 * TQ), 0)
                c = lax.broadcasted_iota(jnp.int32, (TK, 2 * TQ), 1)
                st_ref[...] = jnp.where(r <= jnp.where(c >= TQ, c - TQ, c), st_ref[...], NEG_INF)
            m_prev = m_ref[...]
            m_new = jnp.maximum(m_prev, jnp.max(st_ref[...], axis=0, keepdims=True))
            alpha = jnp.exp2(m_prev - m_new)
            p = jnp.exp2(st_ref[...] - m_new).astype(BF16)
            acc_ref[...] = alpha * acc_ref[...] + _dot(vt_ref[kj], p)
            m_ref[...] = m_new

        scores(0, sa_ref)

        def pair(i, carry, scores=scores, step=step):
            kj = 2 * i
            scores(kj + 1, sb_ref)
            step(kj, sa_ref, False)
            scores(kj + 2, sa_ref)
            step(kj + 1, sb_ref, False)
            return carry

        if qi // 2 > 0:
            lax.fori_loop(0, qi // 2, pair, 0)
        if qi % 2 == 1:
            scores(qi, sb_ref)
            step(qi - 1, sa_ref, False)
            step(qi, sb_ref, True)
        else:
            step(qi, sa_ref, True)

        acc = acc_ref[...]
        o = acc[0:DV] * (1.0 / acc[DV:DV + 1])
        ot = o[:, :TQ] - lam * o[:, TQ:]
        o_ref[qi * TQ:(qi + 1) * TQ, :] = _out_norm(ot.T, og_ref[...], lam_init).astype(BF16)


def _attn_prompt(q, kb, vb, lq1, lk1, lq2, lk2, og, lam_init, batch, seq):
    q3 = q.reshape(batch, seq, D_QK)
    k3 = kb.reshape(batch, seq, D_QK)
    v3 = vb.reshape(batch, seq, D_B)
    vec = _const_spec((1, DK))
    head_spec = pl.BlockSpec((None, seq, LANES), lambda b, h: (b, 0, h))
    out = pl.pallas_call(
        functools.partial(_attn_prompt_kernel, lam_init=lam_init, seq=seq),
        grid=(batch, H_B),
        in_specs=[head_spec, head_spec, head_spec, vec, vec, vec, vec, _const_spec((1, DV))],
        out_specs=head_spec,
        out_shape=jax.ShapeDtypeStruct((batch, seq, D_B), BF16),
        scratch_shapes=[
            pltpu.VMEM((seq // TK, DVA, TK), BF16),
            pltpu.VMEM((TK, 2 * TQ), F32),
            pltpu.VMEM((TK, 2 * TQ), F32),
            pltpu.VMEM((1, 2 * TQ), F32),
            pltpu.VMEM((DVA, 2 * TQ), F32),
        ],
        compiler_params=_params(("parallel", "parallel")),
        name="attn_prompt",
    )(q3, k3, v3, lq1, lk1, lq2, lk2, og)
    return out.reshape(batch * seq, D_B)


def _attn_sample_kernel(pt_ref, q_ref, kn_ref, vn_ref, lq1_ref, lk1_ref, lq2_ref, lk2_ref, og_ref, ck_hbm, cv_hbm,
                        o_ref, kbuf_ref, vbuf_ref, s_ref, sems, *, layer, n_pages, n_new, lam_init):
    b = pl.program_id(0)
    n_b = pl.num_programs(0)

    def pages(bb, slot):
        copies = []
        for p in range(n_pages):
            page_id = pt_ref[bb * n_pages + p]
            copies.append(pltpu.make_async_copy(ck_hbm.at[layer, page_id], kbuf_ref.at[slot, p], sems.at[slot]))
            copies.append(pltpu.make_async_copy(cv_hbm.at[layer, page_id], vbuf_ref.at[slot, p], sems.at[slot]))
        return copies

    @pl.when(b == 0)
    def _():
        for ahead in range(PAGE_SLOTS - 1):
            for cp in pages(ahead, ahead):
                cp.start()

    nxt = b + (PAGE_SLOTS - 1)

    @pl.when(nxt < n_b)
    def _():
        for cp in pages(nxt, lax.rem(nxt, PAGE_SLOTS)):
            cp.start()

    slot = lax.rem(b, PAGE_SLOTS)
    for cp in pages(b, slot):
        cp.wait()
    k_pages = [kbuf_ref.at[slot, p] for p in range(n_pages)]
    v_pages = [vbuf_ref.at[slot, p] for p in range(n_pages)]
    n_rows = H_B * 2 * n_new
    n_cols = kbuf_ref.shape[2]

    q = q_ref[...].astype(F32)
    lane = lax.broadcasted_iota(jnp.int32, (n_new, LANES), 1)
    parts = []
    for h in range(H_B):
        qh = q[:, h * LANES:(h + 1) * LANES]
        parts.append(jnp.where(lane < DK, qh, 0.0))
        parts.append(jnp.where(lane >= DK, qh, 0.0))
    qs = jnp.concatenate(parts, axis=0).astype(BF16)

    def valid_mask(cols, causal):
        r = lax.broadcasted_iota(jnp.int32, (n_rows, cols), 0)
        c = lax.broadcasted_iota(jnp.int32, (n_rows, cols), 1)
        valid = (c & (H_B - 1)) == lax.shift_right_logical(r, int(math.log2(2 * n_new)))
        if causal:
            valid = valid & (lax.shift_right_logical(c, int(math.log2(H_B))) <= (r & (n_new - 1)))
        return valid

    valid = valid_mask(n_cols, False)
    mx = jnp.full((n_rows, n_cols), NEG_INF, F32)
    for p in range(n_pages):
        s = jnp.where(valid, _dot_nt(qs, k_pages[p][...].astype(BF16)), NEG_INF)
        s_ref[:, p * n_cols:(p + 1) * n_cols] = s
        mx = jnp.maximum(mx, s)
    s_new = jnp.where(valid_mask(n_new * H_B, True), _dot_nt(qs, kn_ref[...]), NEG_INF)
    m = jnp.maximum(jnp.max(mx, axis=-1, keepdims=True), jnp.max(s_new, axis=-1, keepdims=True))

    p_new = jnp.exp2(s_new - m)
    acc = _dot(p_new.astype(BF16), vn_ref[...])
    lsum = jnp.zeros((n_rows, n_cols), F32)
    for p in range(n_pages):
        pp = jnp.exp2(s_ref[:, p * n_cols:(p + 1) * n_cols] - m)
        lsum = lsum + pp
        acc = acc + _dot(pp.astype(BF16), v_pages[p][...].astype(BF16))
    l = jnp.sum(lsum, axis=-1, keepdims=True) + jnp.sum(p_new, axis=-1, keepdims=True)

    lam = _lam(lq1_ref, lk1_ref, lq2_ref, lk2_ref, lam_init)
    o = acc / l
    for h in range(H_B):
        base = h * 2 * n_new
        oh = o[base:base + n_new] - lam * o[base + n_new:base + 2 * n_new]
        o_ref[:, h * DV:(h + 1) * DV] = _out_norm(oh, og_ref[...], lam_init)


def _attn_sample(q, kb, vb, cache_k, cache_v, page_table, layer, lq1, lk1, lq2, lk2, og, lam_init, dec_batch, n_new):
    n_pages = page_table.shape[1]
    page = cache_k.shape[2]
    n_pool = cache_k.shape[1]
    ck = cache_k.reshape(DEPTH, n_pool, page * H_B, 2 * DK)
    cv = cache_v.reshape(DEPTH, n_pool, page * H_B, DV)
    q3 = q.reshape(dec_batch, n_new, D_QK)
    kn = kb.reshape(dec_batch, n_new * H_B, 2 * DK)
    vn = vb.reshape(dec_batch, n_new * H_B, DV)
    n_rows = H_B * 2 * n_new
    vec = pl.BlockSpec((1, DK), lambda b, pt: (0, 0))

    any_spec = pl.BlockSpec(memory_space=pl.ANY)
    grid_spec = pltpu.PrefetchScalarGridSpec(
        num_scalar_prefetch=1,
        grid=(dec_batch,),
        in_specs=[
            pl.BlockSpec((None, n_new, D_QK), lambda b, pt: (b, 0, 0)),
            pl.BlockSpec((None, n_new * H_B, LANES), lambda b, pt: (b, 0, 0)),
            pl.BlockSpec((None, n_new * H_B, LANES), lambda b, pt: (b, 0, 0)),
            vec, vec, vec, vec,
            pl.BlockSpec((1, DV), lambda b, pt: (0, 0)),
            any_spec, any_spec,
        ],
        out_specs=pl.BlockSpec((None, n_new, D_B), lambda b, pt: (b, 0, 0)),
        scratch_shapes=[
            pltpu.VMEM((PAGE_SLOTS, n_pages, page * H_B, LANES), F32),
            pltpu.VMEM((PAGE_SLOTS, n_pages, page * H_B, LANES), F32),
            pltpu.VMEM((n_rows, n_pages * page * H_B), F32),
            pltpu.SemaphoreType.DMA((PAGE_SLOTS,)),
        ],
    )
    out = pl.pallas_call(
        functools.partial(_attn_sample_kernel, layer=layer, n_pages=n_pages, n_new=n_new, lam_init=lam_init),
        grid_spec=grid_spec,
        out_shape=jax.ShapeDtypeStruct((dec_batch, n_new, D_B), F32),
        compiler_params=_params(("arbitrary",)),
        name="attn_sample",
    )(page_table.reshape(-1), q3, kn, vn, lq1, lk1, lq2, lk2, og, ck, cv)
    return out.reshape(dec_batch * n_new, D_B)


def _merge_router_kernel(*refs, grouped, n_sub):
    if grouped:
        @pl.when(pl.program_id(0) == 0)
        def _():
            refs[-1][...] = jnp.zeros(refs[-1].shape, F32)

    cnt = refs[-1][...] if grouped else None
    for sub in range(n_sub):
        cnt = _merge_router_rows(refs, slice(sub * TM_PROJ, (sub + 1) * TM_PROJ), grouped, cnt)
    if grouped:
        refs[-1][...] = cnt
        refs[-2][...] = cnt[:, :ROUTER_LANES]


def _merge_router_rows(refs, rows, grouped, cnt):
    a_ref, b_ref, x_ref, wo_ref, g2_ref, rhi_ref, rlo_ref = refs[:7]
    y = _dot(a_ref[rows, :].astype(BF16), wo_ref[0:D_A, :]) + _dot(b_ref[rows, :].astype(BF16), wo_ref[D_A:, :])
    x1 = x_ref[rows, :] + y
    h = _rms(x1) * g2_ref[...]
    hb = h.astype(BF16)
    hlo = (h - hb.astype(F32)).astype(BF16)
    rhi = rhi_ref[...]
    logits = _dot(hb, rhi) + _dot(hlo, rhi) + _dot(hb, rlo_ref[...])

    lt = logits.T[0:ROUTER_ROWS, :]
    row_i = lax.broadcasted_iota(jnp.int32, lt.shape, 0)
    row = row_i.astype(F32)
    row_group = lax.shift_right_logical(row_i, int(math.log2(E_PER))).astype(F32)
    big = float(ROUTER_ROWS)

    def first_row(mask):
        return jnp.min(jnp.where(mask, row, big), axis=0, keepdims=True)

    is_g = (row_i >= GROUP_LANE0) & (row_i < GROUP_LANE0 + N_GROUPS)
    lg = jnp.where(is_g, lt, NEG_INF)
    eg = jnp.where(is_g, jnp.exp(lg - jnp.max(lg, axis=0, keepdims=True)), 0.0)
    pg = eg / jnp.sum(eg, axis=0, keepdims=True)
    gw = jnp.max(pg, axis=0, keepdims=True)
    gi = first_row(is_g & (pg == gw)) - GROUP_LANE0
    sel = (row_i < N_EXP) & (row_group == gi)
    le = jnp.where(sel, lt, NEG_INF)
    ee = jnp.where(sel, jnp.exp(le - jnp.max(le, axis=0, keepdims=True)), 0.0)
    pe = ee / jnp.sum(ee, axis=0, keepdims=True)
    t1 = jnp.max(jnp.where(sel, pe, -1.0), axis=0, keepdims=True)
    i1 = first_row(sel & (pe == t1))
    rest = sel & (row != i1)
    t2 = jnp.max(jnp.where(rest, pe, -1.0), axis=0, keepdims=True)
    i2 = first_row(rest & (pe == t2))
    den = t1 + t2
    gates_t = jnp.where(row == i1, gw * (t1 / den), jnp.where(row == i2, gw * (t2 / den), 0.0))

    if not grouped:
        xg_ref, h_ref = refs[7:]
        h_ref[rows, :] = hb
    else:
        utri_ref, xg_ref, counts_ref, cnt_ref = refs[7:]
        row8 = lax.broadcasted_iota(jnp.int32, (SUBLANES, lt.shape[1]), 0).astype(F32)
        onehot = jnp.where(row8 == gi, 1.0, 0.0)
        earlier = _dot(onehot.astype(BF16), utri_ref[...])
        rank = jnp.sum(onehot * (earlier + cnt), axis=0, keepdims=True)
        cnt = cnt + jnp.sum(onehot, axis=1, keepdims=True)
        gates_t = gates_t + jnp.where(row == META_GROUP_LANE, gi, jnp.where(row == META_RANK_LANE, rank, 0.0))
    pad = jnp.zeros((ROUTER_LANES - ROUTER_ROWS, gates_t.shape[1]), F32)
    xg_ref[rows, :D_MODEL] = x1
    xg_ref[rows, D_MODEL:] = jnp.concatenate([gates_t, pad], axis=0).T
    return cnt


def _merge_router(aout, bout, x2d, w_out_b, layer, g2, rhi, rlo, grouped):
    n = x2d.shape[0]
    n_sub = 2
    tm = n_sub * TM_PROJ
    row_spec = lambda w: pl.BlockSpec((tm, w), lambda i: (i, 0))
    in_specs = [
        row_spec(D_A), row_spec(D_B), row_spec(D_MODEL),
        pl.BlockSpec((None, D_MODEL, D_MODEL), lambda i: (layer, 0, 0)),
        _const_spec((1, D_MODEL)),
        _const_spec((D_MODEL, ROUTER_LANES)),
        _const_spec((D_MODEL, ROUTER_LANES)),
    ]
    args = [aout, bout, x2d, w_out_b, g2, rhi, rlo]
    xg_shape = jax.ShapeDtypeStruct((n, XG_W), F32)
    if grouped:
        r = jnp.arange(TM_PROJ)
        args.append((r[:, None] < r[None, :]).astype(BF16))
        in_specs.append(_const_spec((TM_PROJ, TM_PROJ)))
        out_specs = (row_spec(XG_W), _const_spec((8, ROUTER_LANES)))
        out_shape = (xg_shape, jax.ShapeDtypeStruct((8, ROUTER_LANES), F32))
        scratch = [pltpu.VMEM((8, TM_PROJ), F32)]
    else:
        out_specs = (row_spec(XG_W), row_spec(D_MODEL))
        out_shape = (xg_shape, jax.ShapeDtypeStruct((n, D_MODEL), BF16))
        scratch = []
    return pl.pallas_call(
        functools.partial(_merge_router_kernel, grouped=grouped, n_sub=n_sub),
        grid=(n // tm,),
        in_specs=in_specs,
        out_specs=out_specs,
        out_shape=out_shape,
        scratch_shapes=scratch,
        compiler_params=_params(("arbitrary",)),
        name="merge_router",
    )(*args)


def _moe_kernel(h_ref, gates_ref, x1_ref, wg_ref, wu_ref, wd_ref, o_ref):
    step = pl.program_id(1)

    @pl.when(step == 0)
    def _():
        o_ref[...] = x1_ref[...]

    h = h_ref[...]
    gates = gates_ref[...]
    lane = lax.broadcasted_iota(jnp.int32, gates.shape, 1)
    acc = o_ref[...]
    for j in range(EXPERTS_PER_STEP):
        ge = jnp.sum(jnp.where(lane == step * EXPERTS_PER_STEP + j, gates, 0.0), axis=-1, keepdims=True)
        a = jax.nn.silu(_dot(h, wg_ref[j])) * _dot(h, wu_ref[j]) * ge
        acc = acc + _dot(a.astype(BF16), wd_ref[j])
    o_ref[...] = acc


def _moe(h, xg, wg, wu, wd, layer):
    n = h.shape[0]
    tm = min(TM_MOE, n)
    return pl.pallas_call(
        _moe_kernel,
        grid=(n // tm, N_EXP // EXPERTS_PER_STEP),
        in_specs=[
            pl.BlockSpec((tm, D_MODEL), lambda i, e: (i, 0)),
            pl.BlockSpec((tm, ROUTER_LANES), lambda i, e: (i, D_MODEL // ROUTER_LANES)),
            pl.BlockSpec((tm, D_MODEL), lambda i, e: (i, 0)),
            pl.BlockSpec((None, EXPERTS_PER_STEP, D_MODEL, D_E), lambda i, e: (layer, e, 0, 0)),
            pl.BlockSpec((None, EXPERTS_PER_STEP, D_MODEL, D_E), lambda i, e: (layer, e, 0, 0)),
            pl.BlockSpec((None, EXPERTS_PER_STEP, D_E, D_MODEL), lambda i, e: (layer, e, 0, 0)),
        ],
        out_specs=pl.BlockSpec((tm, D_MODEL), lambda i, e: (i, 0)),
        out_shape=jax.ShapeDtypeStruct((n, D_MODEL), F32),
        compiler_params=_params(("parallel", "arbitrary")),
        name="moe",
    )(h, xg, xg, wg, wu, wd)


SUBLANES = 8


def _row_copies(n_rows, make_copy):
    def issue(k, carry):
        r0 = pl.multiple_of(k * SUBLANES, SUBLANES)
        for u in range(SUBLANES):
            make_copy(r0, u).start()
        return carry

    def drain(k, carry):
        r0 = pl.multiple_of(k * SUBLANES, SUBLANES)
        for u in range(SUBLANES):
            make_copy(r0, u).wait()
        return carry

    lax.fori_loop(0, n_rows // SUBLANES, issue, 0)
    lax.fori_loop(0, n_rows // SUBLANES, drain, 0)


def _dispatch_kernel(pos_ref, tile_end_ref, xg_ref, sorted_hbm, zero_ref, sem, zero_sem, *, tm):
    @pl.when(pl.program_id(0) == 0)
    def _():
        zero_ref[...] = jnp.zeros(zero_ref.shape, F32)
        for g in range(N_GROUPS):
            first = tile_end_ref[g - 1] if g > 0 else 0

            @pl.when(tile_end_ref[g] > first)
            def _():
                start = pl.multiple_of((tile_end_ref[g] - 1) * TM_SORT, TM_SORT)
                fill = pltpu.make_async_copy(zero_ref, sorted_hbm.at[pl.ds(start, TM_SORT), :], zero_sem)
                fill.start()
                fill.wait()

    base = pl.program_id(0) * tm
    _row_copies(tm, lambda r0, u: pltpu.make_async_copy(
        xg_ref.at[pl.ds(r0, SUBLANES), :].at[pl.ds(u, 1), :],
        sorted_hbm.at[pl.ds(pos_ref[base + r0 + u], 1), :], sem))


def _dispatch(pos, tile_end, xg, n_pad):
    n = xg.shape[0]
    tm = min(TM_MOVE, n)
    return pl.pallas_call(
        functools.partial(_dispatch_kernel, tm=tm),
        grid_spec=pltpu.PrefetchScalarGridSpec(
            num_scalar_prefetch=2,
            grid=(n // tm,),
            in_specs=[pl.BlockSpec((tm, XG_W), lambda i, pos, te: (i, 0))],
            out_specs=pl.BlockSpec(memory_space=pl.ANY),
            scratch_shapes=[pltpu.VMEM((TM_SORT, XG_W), F32), pltpu.SemaphoreType.DMA(()), pltpu.SemaphoreType.DMA(())],
        ),
        out_shape=jax.ShapeDtypeStruct((n_pad, XG_W), F32),
        compiler_params=_params(("arbitrary",)),
        name="moe_dispatch",
    )(pos, tile_end, xg)


def _combine_kernel(pos_ref, sorted_hbm, o_ref, sem, *, tm):
    base = pl.program_id(0) * tm
    _row_copies(tm, lambda r0, u: pltpu.make_async_copy(
        sorted_hbm.at[pl.ds(pos_ref[base + r0 + u], 1), :],
        o_ref.at[pl.ds(r0, SUBLANES), :].at[pl.ds(u, 1), :], sem))


def _combine(pos, y_sorted, n):
    tm = min(TM_MOVE, n)
    return pl.pallas_call(
        functools.partial(_combine_kernel, tm=tm),
        grid_spec=pltpu.PrefetchScalarGridSpec(
            num_scalar_prefetch=1,
            grid=(n // tm,),
            in_specs=[pl.BlockSpec(memory_space=pl.ANY)],
            out_specs=pl.BlockSpec((tm, D_MODEL), lambda i, pos: (i, 0)),
            scratch_shapes=[pltpu.SemaphoreType.DMA(())],
        ),
        out_shape=jax.ShapeDtypeStruct((n, D_MODEL), F32),
        compiler_params=_params(("arbitrary",)),
        name="moe_combine",
    )(pos, y_sorted)


def _moe_sorted_kernel(tg_ref, nv_ref, xg_ref, g2_ref, wg_ref, wu_ref, wd_ref, o_ref):
    j = pl.program_id(0)

    @pl.when(j >= nv_ref[0])
    def _():
        o_ref[...] = jnp.zeros(o_ref.shape, F32)

    @pl.when(j < nv_ref[0])
    def _():
        first_expert = tg_ref[j] * E_PER
        x1 = xg_ref[:, :D_MODEL]
        gates = xg_ref[:, D_MODEL:]
        hb = (_rms(x1) * g2_ref[...]).astype(BF16)
        lane = lax.broadcasted_iota(jnp.int32, gates.shape, 1)
        acc = x1
        for e in range(E_PER):
            ge = jnp.sum(jnp.where(lane == first_expert + e, gates, 0.0), axis=-1, keepdims=True)
            a = jax.nn.silu(_dot(hb, wg_ref[e])) * _dot(hb, wu_ref[e]) * ge
            acc = acc + _dot(a.astype(BF16), wd_ref[e])
        o_ref[...] = acc


def _moe_sorted(tile_group, n_valid, xg_sorted, g2, wg, wu, wd, layer):
    n_pad = xg_sorted.shape[0]
    tm = TM_SORT
    by_group = lambda w: w.reshape(DEPTH, N_GROUPS, E_PER, *w.shape[2:])
    last = lambda j, nv: jnp.minimum(j, nv[0] - 1)
    w_spec = lambda a, b: pl.BlockSpec((None, None, E_PER, a, b), lambda j, tg, nv: (layer, tg[last(j, nv)], 0, 0, 0))
    return pl.pallas_call(
        _moe_sorted_kernel,
        grid_spec=pltpu.PrefetchScalarGridSpec(
            num_scalar_prefetch=2,
            grid=(n_pad // tm,),
            in_specs=[
                pl.BlockSpec((tm, XG_W), lambda j, tg, nv: (last(j, nv), 0)),
                pl.BlockSpec((1, D_MODEL), lambda j, tg, nv: (0, 0)),
                w_spec(D_MODEL, D_E), w_spec(D_MODEL, D_E), w_spec(D_E, D_MODEL),
            ],
            out_specs=pl.BlockSpec((tm, D_MODEL), lambda j, tg, nv: (j, 0)),
        ),
        out_shape=jax.ShapeDtypeStruct((n_pad, D_MODEL), F32),
        compiler_params=_params(("arbitrary",)),
        name="moe_sorted",
    )(tile_group, n_valid, xg_sorted, g2, by_group(wg), by_group(wu), by_group(wd))


def _moe_grouped(xg, counts, g2, wg, wu, wd, layer):
    n = xg.shape[0]
    n_pad = n + N_GROUPS * TM_SORT
    n_tiles = n_pad // TM_SORT
    meta = xg[:, D_MODEL + META_GROUP_LANE:D_MODEL + META_RANK_LANE + 1].astype(jnp.int32)
    tiles_per_group = (counts[:N_GROUPS, 0].astype(jnp.int32) + TM_SORT - 1) // TM_SORT
    tile_end = jnp.cumsum(tiles_per_group)
    pos = ((tile_end - tiles_per_group) * TM_SORT)[meta[:, 0]] + meta[:, 1]
    tile_group = jnp.minimum(jnp.sum(jnp.arange(n_tiles)[:, None] >= tile_end[None, :], axis=1), N_GROUPS - 1)
    xg_sorted = _dispatch(pos, tile_end, xg, n_pad)
    y_sorted = _moe_sorted(tile_group.astype(jnp.int32), tile_end[N_GROUPS - 1:], xg_sorted, g2, wg, wu, wd, layer)
    return _combine(pos, y_sorted, n)


def _router_weights(r_g, r_e):
    r = jnp.concatenate([r_e, r_g, jnp.zeros((D_MODEL, ROUTER_LANES - N_EXP - N_GROUPS), F32)], axis=1)
    hi = r.astype(BF16)
    lo = (r - hi.astype(F32)).astype(BF16)
    return hi, lo


def kernel(x_prompt, x_sample, cache_k, cache_v, page_table, norm1_g, w_in, a_vnorm_g, a_ws, a_bs, qn_g, kn_g, lam_q1, lam_k1, lam_q2, lam_k2, b_onorm_g, w_out, norm2_g, router_g, router_e, w_gate, w_up, w_down):
    batch, seq, _ = x_prompt.shape
    dec_batch, n_new, _ = x_sample.shape
    xp = x_prompt.reshape(batch * seq, D_MODEL)
    xs = x_sample.reshape(dec_batch * n_new, D_MODEL)

    ii = jnp.arange(MXU_DIM) // DK
    bd = (ii[:, None] == ii[None, :]).astype(BF16)
    eye = jnp.eye(CHUNK // n_new, dtype=F32)

    w_in_b, w_out_b = w_in.astype(BF16), w_out.astype(BF16)
    wg, wu, wd = w_gate.astype(BF16), w_up.astype(BF16), w_down.astype(BF16)

    rows_p, rows_s = None, None
    for l in range(DEPTH):
        lam_init = 0.8 - 0.6 * math.exp(-0.3 * l)
        rhi, rlo = _router_weights(router_g[l], router_e[l])
        g1 = norm1_g[l][None]
        g2 = norm2_g[l][None]
        avg = a_vnorm_g[l][None]
        qg = jnp.tile(qn_g[l], D_QK // DK)[None]
        kg = jnp.tile(kn_g[l], D_QK // DK)[None]
        og = b_onorm_g[l][None]
        lams = (lam_q1[l][None], lam_k1[l][None], lam_q2[l][None], lam_k2[l][None])
        ws_p = a_ws[l]
        bias_p = jnp.repeat(a_bs[l].T, DH_A, axis=1)
        ws_s = jax.vmap(lambda w: jnp.kron(eye, w[:n_new, :n_new]))(a_ws[l])
        bias_s = jnp.tile(jnp.repeat(a_bs[l][:, :n_new].T, DH_A, axis=1), (CHUNK // n_new, 1))

        aout, q, k_all, kb, v_all, vb = _in_proj(xp, g1, w_in_b, avg, qg, kg, ws_p, bias_p, bd, l, rows_p, False)
        rows_p = (k_all, v_all)
        bout = _attn_prompt(q, kb, vb, *lams, og, lam_init, batch, seq)
        xg, counts = _merge_router(aout, bout, xp, w_out_b, l, g2, rhi, rlo, True)
        xp = _moe_grouped(xg, counts, g2, wg, wu, wd, l)

        aout, q, k_all, kb, v_all, vb, va_all = _in_proj(xs, g1, w_in_b, avg, qg, kg, ws_s, bias_s, bd, l, rows_s, True)
        rows_s = (k_all, v_all, va_all)
        bout = _attn_sample(q, kb, vb, cache_k, cache_v, page_table, l, *lams, og, lam_init, dec_batch, n_new)
        xg, h = _merge_router(aout, bout, xs, w_out_b, l, g2, rhi, rlo, False)
        xs = _moe(h, xg, wg, wu, wd, l)

    return (xp.reshape(batch, seq, D_MODEL), xs.reshape(dec_batch, n_new, D_MODEL),
            rows_p[0].reshape(DEPTH, batch, seq, H_B, 2 * DK), rows_p[1].reshape(DEPTH, batch, seq, H_B, DV),
            rows_s[0].reshape(DEPTH, dec_batch, n_new, H_B, 2 * DK), rows_s[1].reshape(DEPTH, dec_batch, n_new, H_B, DV),
            rows_s[2].reshape(DEPTH, dec_batch, n_new, D_A))
```

```python
import functools
import math

import jax
import jax.numpy as jnp
from jax import lax
from jax.experimental import pallas as pl
from jax.experimental.pallas import tpu as pltpu

F32 = jnp.float32
BF16 = jnp.bfloat16

D_MODEL = 1024
DEPTH = 2
D_A = 512
H_A = 8
DH_A = D_A // H_A
CHUNK = 128
D_B = 512
H_B = 4
DV = D_B // H_B
DK = DV // 2
D_QK = H_B * 2 * DK
D_IN = 2 * D_A + 2 * D_QK + D_B
ATTN_SCALE = DK ** -0.5
LOG2E = math.log2(math.e)
DVA = DV + 16
NEG_INF = -1e30
N_GROUPS = 4
E_PER = 4
N_EXP = N_GROUPS * E_PER
D_E = 256
EPS = 1e-6

LANES = 128
MXU_DIM = 256
VMEM_LIMIT_BYTES = 48 * 1024 * 1024
VMEM_LIMIT_SORTED_MOE_BYTES = 56 * 1024 * 1024

TM_PROJ = 512
TQ = 512
TK = 512
assert TQ == TK
TM_MOE = 1024
EXPERTS_PER_STEP = 2
ROUTER_LANES = 128
GROUP_LANE0 = N_EXP
XG_W = D_MODEL + ROUTER_LANES
META_GROUP_LANE = N_EXP
META_RANK_LANE = N_EXP + 1
TM_SORT = 512
TM_MOVE = 2048
PAGE_SLOTS = 3
ROUTER_ROWS = 24


def _params(sem, vmem_limit_bytes=VMEM_LIMIT_BYTES):
    return pltpu.CompilerParams(dimension_semantics=sem, vmem_limit_bytes=vmem_limit_bytes)


def _const_spec(shape):
    nd = len(shape)
    return pl.BlockSpec(shape, lambda *_: (0,) * nd)


def _rms(x):
    return x * lax.rsqrt(jnp.mean(x * x, axis=-1, keepdims=True) + EPS)


def _dot(a, b):
    return jnp.dot(a, b, preferred_element_type=F32)


def _dot_nt(a, b):
    return lax.dot_general(a, b, (((1,), (1,)), ((), ())), preferred_element_type=F32)


N_PROJ_IN = 9


def _in_proj_kernel(*refs, tm, n_alias, emit_va):
    x_ref, g1_ref, w_ref, avg_ref, qg_ref, kg_ref, ws_ref, bias_ref, bd_ref = refs[:N_PROJ_IN]
    outs = refs[N_PROJ_IN + n_alias:]
    aout_ref, q_ref, k32_ref, kb_ref, v32_ref, vb_ref = outs[:6]
    x = x_ref[...]
    xb = (_rms(x) * g1_ref[...]).astype(BF16)

    a = jax.nn.gelu(_dot(xb, w_ref[:, 0:2 * D_A]))
    u = a[:, :D_A]
    va = _rms(a[:, D_A:]) * avg_ref[...]
    if emit_va:
        outs[6][...] = va
    vab = va.astype(BF16)

    row = lax.broadcasted_iota(jnp.int32, (CHUNK, CHUNK), 0)
    col = lax.broadcasted_iota(jnp.int32, (CHUNK, CHUNK), 1)
    tril = row >= col
    wsm = [jnp.where(tril, ws_ref[h], 0.0).astype(BF16) for h in range(H_A)]
    first_head = lax.broadcasted_iota(jnp.int32, (CHUNK, LANES), 1) < DH_A
    for c in range(tm // CHUNK):
        rows = slice(c * CHUNK, (c + 1) * CHUNK)
        for hp in range(H_A // 2):
            cols = slice(hp * LANES, (hp + 1) * LANES)
            blk = vab[rows, cols]
            s = jnp.where(first_head, _dot(wsm[2 * hp], blk), _dot(wsm[2 * hp + 1], blk))
            s = s + bias_ref[:, cols]
            aout_ref[rows, cols] = (u[rows, cols] * s).astype(BF16)

    bd = bd_ref[...]

    def group_norm(z, g):
        sq = (z * z).astype(BF16)
        ss = jnp.concatenate([_dot(sq[:, i * MXU_DIM:(i + 1) * MXU_DIM], bd) for i in range(D_QK // MXU_DIM)], axis=1)
        return z * lax.rsqrt(ss * (1.0 / DK) + EPS) * g

    qn = group_norm(_dot(xb, w_ref[:, 2 * D_A:2 * D_A + D_QK]), qg_ref[...])
    q_ref[...] = (qn * (ATTN_SCALE * LOG2E)).astype(BF16)
    kn = group_norm(_dot(xb, w_ref[:, 2 * D_A + D_QK:2 * D_A + 2 * D_QK]), kg_ref[...])
    kb_ref[...] = kn.astype(BF16)
    v = _dot(xb, w_ref[:, 2 * D_A + 2 * D_QK:D_IN])
    vb_ref[...] = v.astype(BF16)
    for h in range(H_B):
        k32_ref[pl.ds(h, tm, stride=H_B), :] = kn[:, h * LANES:(h + 1) * LANES]
        v32_ref[pl.ds(h, tm, stride=H_B), :] = v[:, h * LANES:(h + 1) * LANES]


def _in_proj(x2d, g1, w_in_b, avg, qg, kg, ws, bias, bd, layer, prev, emit_va):
    n = x2d.shape[0]
    tm = TM_PROJ
    row_spec = lambda w: pl.BlockSpec((tm, w), lambda i: (i, 0))
    rows4 = jax.ShapeDtypeStruct((DEPTH, n * H_B, LANES), F32)
    rows4_spec = pl.BlockSpec((None, tm * H_B, LANES), lambda i: (layer, i, 0))
    out_shapes = [
        jax.ShapeDtypeStruct((n, D_A), BF16),
        jax.ShapeDtypeStruct((n, D_QK), BF16),
        rows4,
        jax.ShapeDtypeStruct((n, D_QK), BF16),
        rows4,
        jax.ShapeDtypeStruct((n, D_B), BF16),
    ]
    out_specs = [row_spec(D_A), row_spec(D_QK), rows4_spec, row_spec(D_QK), rows4_spec, row_spec(D_B)]
    if emit_va:
        out_shapes.append(jax.ShapeDtypeStruct((DEPTH, n, D_A), F32))
        out_specs.append(pl.BlockSpec((None, tm, D_A), lambda i: (layer, i, 0)))
    prev = () if prev is None else tuple(prev)
    alias_out = (2, 4, 6)[:len(prev)]
    return pl.pallas_call(
        functools.partial(_in_proj_kernel, tm=tm, n_alias=len(prev), emit_va=emit_va),
        grid=(n // tm,),
        in_specs=[
            row_spec(D_MODEL),
            _const_spec((1, D_MODEL)),
            pl.BlockSpec((None, D_MODEL, D_IN), lambda i: (layer, 0, 0)),
            _const_spec((1, D_A)),
            _const_spec((1, D_QK)),
            _const_spec((1, D_QK)),
            _const_spec((H_A, CHUNK, CHUNK)),
            _const_spec((CHUNK, D_A)),
            _const_spec((MXU_DIM, MXU_DIM)),
        ] + [pl.BlockSpec(memory_space=pl.ANY)] * len(prev),
        out_specs=tuple(out_specs),
        out_shape=tuple(out_shapes),
        input_output_aliases={N_PROJ_IN + j: o for j, o in enumerate(alias_out)},
        compiler_params=_params(("parallel",)),
        name="in_proj",
    )(x2d, g1, w_in_b, avg, qg, kg, ws, bias, bd, *prev)


def _lam(lq1_ref, lk1_ref, lq2_ref, lk2_ref, lam_init):
    s1 = jnp.sum(lq1_ref[...] * lk1_ref[...], axis=-1, keepdims=True)
    s2 = jnp.sum(lq2_ref[...] * lk2_ref[...], axis=-1, keepdims=True)
    return jnp.exp(s1) - jnp.exp(s2) + lam_init


def _out_norm(o, og, lam_init):
    return _rms(o) * og * (1.0 - lam_init)


def _attn_prompt_kernel(q_ref, k_ref, v_ref, lq1_ref, lk1_ref, lq2_ref, lk2_ref, og_ref, o_ref,
                        vt_ref, sa_ref, sb_ref, m_ref, acc_ref, *, lam_init, seq):
    ones = jnp.ones((DVA - DV, TK), BF16)
    for j in range(seq // TK):
        vt_ref[j, 0:DV, :] = v_ref[j * TK:(j + 1) * TK, :].astype(F32).T.astype(BF16)
        vt_ref[j, DV:DVA, :] = ones
    lam = _lam(lq1_ref, lk1_ref, lq2_ref, lk2_ref, lam_init)
    lane = lax.broadcasted_iota(jnp.int32, (TQ, LANES), 1)

    for qi in range(seq // TQ):
        q = q_ref[qi * TQ:(qi + 1) * TQ, :]
        zero = jnp.zeros_like(q)
        qs = jnp.concatenate([jnp.where(lane < DK, q, zero), jnp.where(lane >= DK, q, zero)], axis=0)

        m_ref[...] = jnp.full(m_ref.shape, NEG_INF, F32)
        acc_ref[...] = jnp.zeros(acc_ref.shape, F32)

        def scores(kj, st_ref, qs=qs):
            start = kj * TK if isinstance(kj, int) else pl.multiple_of(kj * TK, TK)
            st_ref[...] = _dot_nt(k_ref[pl.ds(start, TK), :], qs)

        def step(kj, st_ref, diagonal):
            if diagonal:
                r = lax.broadcasted_iota(jnp.int32, (TK, 2 * TQ), 0)
                c = lax.broadcasted_iota(jnp.int32, (TK, 2 * TQ), 1)
                st_ref[...] = jnp.where(r <= jnp.where(c >= TQ, c - TQ, c), st_ref[...], NEG_INF)
            m_prev = m_ref[...]
            m_new = jnp.maximum(m_prev, jnp.max(st_ref[...], axis=0, keepdims=True))
            alpha = jnp.exp2(m_prev - m_new)
            p = jnp.exp2(st_ref[...] - m_new).astype(BF16)
            acc_ref[...] = alpha * acc_ref[...] + _dot(vt_ref[kj], p)
            m_ref[...] = m_new

        scores(0, sa_ref)

        def pair(i, carry, scores=scores, step=step):
            kj = 2 * i
            scores(kj + 1, sb_ref)
            step(kj, sa_ref, False)
            scores(kj + 2, sa_ref)
            step(kj + 1, sb_ref, False)
            return carry

        if qi // 2 > 0:
            lax.fori_loop(0, qi // 2, pair, 0)
        if qi % 2 == 1:
            scores(qi, sb_ref)
            step(qi - 1, sa_ref, False)
            step(qi, sb_ref, True)
        else:
            step(qi, sa_ref, True)

        acc = acc_ref[...]
        o = acc[0:DV] * (1.0 / acc[DV:DV + 1])
        ot = o[:, :TQ] - lam * o[:, TQ:]
        o_ref[qi * TQ:(qi + 1) * TQ, :] = _out_norm(ot.T, og_ref[...], lam_init).astype(BF16)


def _attn_prompt(q, kb, vb, lq1, lk1, lq2, lk2, og, lam_init, batch, seq):
    q3 = q.reshape(batch, seq, D_QK)
    k3 = kb.reshape(batch, seq, D_QK)
    v3 = vb.reshape(batch, seq, D_B)
    vec = _const_spec((1, DK))
    head_spec = pl.BlockSpec((None, seq, LANES), lambda b, h: (b, 0, h))
    out = pl.pallas_call(
        functools.partial(_attn_prompt_kernel, lam_init=lam_init, seq=seq),
        grid=(batch, H_B),
        in_specs=[head_spec, head_spec, head_spec, vec, vec, vec, vec, _const_spec((1, DV))],
        out_specs=head_spec,
        out_shape=jax.ShapeDtypeStruct((batch, seq, D_B), BF16),
        scratch_shapes=[
            pltpu.VMEM((seq // TK, DVA, TK), BF16),
            pltpu.VMEM((TK, 2 * TQ), F32),
            pltpu.VMEM((TK, 2 * TQ), F32),
            pltpu.VMEM((1, 2 * TQ), F32),
            pltpu.VMEM((DVA, 2 * TQ), F32),
        ],
        compiler_params=_params(("parallel", "parallel")),
        name="attn_prompt",
    )(q3, k3, v3, lq1, lk1, lq2, lk2, og)
    return out.reshape(batch * seq, D_B)


def _attn_sample_kernel(pt_ref, q_ref, kn_ref, vn_ref, lq1_ref, lk1_ref, lq2_ref, lk2_ref, og_ref, ck_hbm, cv_hbm,
                        o_ref, kbuf_ref, vbuf_ref, s_ref, sems, *, layer, n_pages, n_new, lam_init):
    b = pl.program_id(0)
    n_b = pl.num_programs(0)

    def pages(bb, slot):
        copies = []
        for p in range(n_pages):
            page_id = pt_ref[bb * n_pages + p]
            copies.append(pltpu.make_async_copy(ck_hbm.at[layer, page_id], kbuf_ref.at[slot, p], sems.at[slot]))
            copies.append(pltpu.make_async_copy(cv_hbm.at[layer, page_id], vbuf_ref.at[slot, p], sems.at[slot]))
        return copies

    @pl.when(b == 0)
    def _():
        for ahead in range(PAGE_SLOTS - 1):
            for cp in pages(ahead, ahead):
                cp.start()

    nxt = b + (PAGE_SLOTS - 1)

    @pl.when(nxt < n_b)
    def _():
        for cp in pages(nxt, lax.rem(nxt, PAGE_SLOTS)):
            cp.start()

    slot = lax.rem(b, PAGE_SLOTS)
    for cp in pages(b, slot):
        cp.wait()
    k_pages = [kbuf_ref.at[slot, p] for p in range(n_pages)]
    v_pages = [vbuf_ref.at[slot, p] for p in range(n_pages)]
    n_rows = H_B * 2 * n_new
    n_cols = kbuf_ref.shape[2]

    q = q_ref[...].astype(F32)
    lane = lax.broadcasted_iota(jnp.int32, (n_new, LANES), 1)
    parts = []
    for h in range(H_B):
        qh = q[:, h * LANES:(h + 1) * LANES]
        parts.append(jnp.where(lane < DK, qh, 0.0))
        parts.append(jnp.where(lane >= DK, qh, 0.0))
    qs = jnp.concatenate(parts, axis=0).astype(BF16)

    def valid_mask(cols, causal):
        r = lax.broadcasted_iota(jnp.int32, (n_rows, cols), 0)
        c = lax.broadcasted_iota(jnp.int32, (n_rows, cols), 1)
        valid = (c & (H_B - 1)) == lax.shift_right_logical(r, int(math.log2(2 * n_new)))
        if causal:
            valid = valid & (lax.shift_right_logical(c, int(math.log2(H_B))) <= (r & (n_new - 1)))
        return valid

    valid = valid_mask(n_cols, False)
    mx = jnp.full((n_rows, n_cols), NEG_INF, F32)
    for p in range(n_pages):
        s = jnp.where(valid, _dot_nt(qs, k_pages[p][...].astype(BF16)), NEG_INF)
        s_ref[:, p * n_cols:(p + 1) * n_cols] = s
        mx = jnp.maximum(mx, s)
    s_new = jnp.where(valid_mask(n_new * H_B, True), _dot_nt(qs, kn_ref[...]), NEG_INF)
    m = jnp.maximum(jnp.max(mx, axis=-1, keepdims=True), jnp.max(s_new, axis=-1, keepdims=True))

    p_new = jnp.exp2(s_new - m)
    acc = _dot(p_new.astype(BF16), vn_ref[...])
    lsum = jnp.zeros((n_rows, n_cols), F32)
    for p in range(n_pages):
        pp = jnp.exp2(s_ref[:, p * n_cols:(p + 1) * n_cols] - m)
        lsum = lsum + pp
        acc = acc + _dot(pp.astype(BF16), v_pages[p][...].astype(BF16))
    l = jnp.sum(lsum, axis=-1, keepdims=True) + jnp.sum(p_new, axis=-1, keepdims=True)

    lam = _lam(lq1_ref, lk1_ref, lq2_ref, lk2_ref, lam_init)
    o = acc / l
    for h in range(H_B):
        base = h * 2 * n_new
        oh = o[base:base + n_new] - lam * o[base + n_new:base + 2 * n_new]
        o_ref[:, h * DV:(h + 1) * DV] = _out_norm(oh, og_ref[...], lam_init)


def _attn_sample(q, kb, vb, cache_k, cache_v, page_table, layer, lq1, lk1, lq2, lk2, og, lam_init, dec_batch, n_new):
    n_pages = page_table.shape[1]
    page = cache_k.shape[2]
    n_pool = cache_k.shape[1]
    ck = cache_k.reshape(DEPTH, n_pool, page * H_B, 2 * DK)
    cv = cache_v.reshape(DEPTH, n_pool, page * H_B, DV)
    q3 = q.reshape(dec_batch, n_new, D_QK)
    kn = kb.reshape(dec_batch, n_new * H_B, 2 * DK)
    vn = vb.reshape(dec_batch, n_new * H_B, DV)
    n_rows = H_B * 2 * n_new
    vec = pl.BlockSpec((1, DK), lambda b, pt: (0, 0))

    any_spec = pl.BlockSpec(memory_space=pl.ANY)
    grid_spec = pltpu.PrefetchScalarGridSpec(
        num_scalar_prefetch=1,
        grid=(dec_batch,),
        in_specs=[
            pl.BlockSpec((None, n_new, D_QK), lambda b, pt: (b, 0, 0)),
            pl.BlockSpec((None, n_new * H_B, LANES), lambda b, pt: (b, 0, 0)),
            pl.BlockSpec((None, n_new * H_B, LANES), lambda b, pt: (b, 0, 0)),
            vec, vec, vec, vec,
            pl.BlockSpec((1, DV), lambda b, pt: (0, 0)),
            any_spec, any_spec,
        ],
        out_specs=pl.BlockSpec((None, n_new, D_B), lambda b, pt: (b, 0, 0)),
        scratch_shapes=[
            pltpu.VMEM((PAGE_SLOTS, n_pages, page * H_B, LANES), F32),
            pltpu.VMEM((PAGE_SLOTS, n_pages, page * H_B, LANES), F32),
            pltpu.VMEM((n_rows, n_pages * page * H_B), F32),
            pltpu.SemaphoreType.DMA((PAGE_SLOTS,)),
        ],
    )
    out = pl.pallas_call(
        functools.partial(_attn_sample_kernel, layer=layer, n_pages=n_pages, n_new=n_new, lam_init=lam_init),
        grid_spec=grid_spec,
        out_shape=jax.ShapeDtypeStruct((dec_batch, n_new, D_B), F32),
        compiler_params=_params(("arbitrary",)),
        name="attn_sample",
    )(page_table.reshape(-1), q3, kn, vn, lq1, lk1, lq2, lk2, og, ck, cv)
    return out.reshape(dec_batch * n_new, D_B)


def _merge_router_kernel(*refs, grouped, n_sub):
    if grouped:
        @pl.when(pl.program_id(0) == 0)
        def _():
            refs[-1][...] = jnp.zeros(refs[-1].shape, F32)

    cnt = refs[-1][...] if grouped else None
    for sub in range(n_sub):
        cnt = _merge_router_rows(refs, slice(sub * TM_PROJ, (sub + 1) * TM_PROJ), grouped, cnt)
    if grouped:
        refs[-1][...] = cnt
        refs[-2][...] = cnt[:, :ROUTER_LANES]


def _merge_router_rows(refs, rows, grouped, cnt):
    a_ref, b_ref, x_ref, wo_ref, g2_ref, rhi_ref, rlo_ref = refs[:7]
    y = _dot(a_ref[rows, :].astype(BF16), wo_ref[0:D_A, :]) + _dot(b_ref[rows, :].astype(BF16), wo_ref[D_A:, :])
    x1 = x_ref[rows, :] + y
    h = _rms(x1) * g2_ref[...]
    hb = h.astype(BF16)
    hlo = (h - hb.astype(F32)).astype(BF16)
    rhi = rhi_ref[...]
    logits = _dot(hb, rhi) + _dot(hlo, rhi) + _dot(hb, rlo_ref[...])

    lt = logits.T[0:ROUTER_ROWS, :]
    row_i = lax.broadcasted_iota(jnp.int32, lt.shape, 0)
    row = row_i.astype(F32)
    row_group = lax.shift_right_logical(row_i, int(math.log2(E_PER))).astype(F32)
    big = float(ROUTER_ROWS)

    def first_row(mask):
        return jnp.min(jnp.where(mask, row, big), axis=0, keepdims=True)

    is_g = (row_i >= GROUP_LANE0) & (row_i < GROUP_LANE0 + N_GROUPS)
    lg = jnp.where(is_g, lt, NEG_INF)
    eg = jnp.where(is_g, jnp.exp(lg - jnp.max(lg, axis=0, keepdims=True)), 0.0)
    pg = eg / jnp.sum(eg, axis=0, keepdims=True)
    gw = jnp.max(pg, axis=0, keepdims=True)
    gi = first_row(is_g & (pg == gw)) - GROUP_LANE0
    sel = (row_i < N_EXP) & (row_group == gi)
    le = jnp.where(sel, lt, NEG_INF)
    ee = jnp.where(sel, jnp.exp(le - jnp.max(le, axis=0, keepdims=True)), 0.0)
    pe = ee / jnp.sum(ee, axis=0, keepdims=True)
    t1 = jnp.max(jnp.where(sel, pe, -1.0), axis=0, keepdims=True)
    i1 = first_row(sel & (pe == t1))
    rest = sel & (row != i1)
    t2 = jnp.max(jnp.where(rest, pe, -1.0), axis=0, keepdims=True)
    i2 = first_row(rest & (pe == t2))
    den = t1 + t2
    gates_t = jnp.where(row == i1, gw * (t1 / den), jnp.where(row == i2, gw * (t2 / den), 0.0))

    if not grouped:
        xg_ref, h_ref = refs[7:]
        h_ref[rows, :] = hb
    else:
        utri_ref, xg_ref, counts_ref, cnt_ref = refs[7:]
        row8 = lax.broadcasted_iota(jnp.int32, (SUBLANES, lt.shape[1]), 0).astype(F32)
        onehot = jnp.where(row8 == gi, 1.0, 0.0)
        earlier = _dot(onehot.astype(BF16), utri_ref[...])
        rank = jnp.sum(onehot * (earlier + cnt), axis=0, keepdims=True)
        cnt = cnt + jnp.sum(onehot, axis=1, keepdims=True)
        gates_t = gates_t + jnp.where(row == META_GROUP_LANE, gi, jnp.where(row == META_RANK_LANE, rank, 0.0))
    pad = jnp.zeros((ROUTER_LANES - ROUTER_ROWS, gates_t.shape[1]), F32)
    xg_ref[rows, :D_MODEL] = x1
    xg_ref[rows, D_MODEL:] = jnp.concatenate([gates_t, pad], axis=0).T
    return cnt


def _merge_router(aout, bout, x2d, w_out_b, layer, g2, rhi, rlo, grouped):
    n = x2d.shape[0]
    n_sub = 2
    tm = n_sub * TM_PROJ
    row_spec = lambda w: pl.BlockSpec((tm, w), lambda i: (i, 0))
    in_specs = [
        row_spec(D_A), row_spec(D_B), row_spec(D_MODEL),
        pl.BlockSpec((None, D_MODEL, D_MODEL), lambda i: (layer, 0, 0)),
        _const_spec((1, D_MODEL)),
        _const_spec((D_MODEL, ROUTER_LANES)),
        _const_spec((D_MODEL, ROUTER_LANES)),
    ]
    args = [aout, bout, x2d, w_out_b, g2, rhi, rlo]
    xg_shape = jax.ShapeDtypeStruct((n, XG_W), F32)
    if grouped:
        r = jnp.arange(TM_PROJ)
        args.append((r[:, None] < r[None, :]).astype(BF16))
        in_specs.append(_const_spec((TM_PROJ, TM_PROJ)))
        out_specs = (row_spec(XG_W), _const_spec((8, ROUTER_LANES)))
        out_shape = (xg_shape, jax.ShapeDtypeStruct((8, ROUTER_LANES), F32))
        scratch = [pltpu.VMEM((8, TM_PROJ), F32)]
    else:
        out_specs = (row_spec(XG_W), row_spec(D_MODEL))
        out_shape = (xg_shape, jax.ShapeDtypeStruct((n, D_MODEL), BF16))
        scratch = []
    return pl.pallas_call(
        functools.partial(_merge_router_kernel, grouped=grouped, n_sub=n_sub),
        grid=(n // tm,),
        in_specs=in_specs,
        out_specs=out_specs,
        out_shape=out_shape,
        scratch_shapes=scratch,
        compiler_params=_params(("arbitrary",)),
        name="merge_router",
    )(*args)


def _moe_kernel(h_ref, gates_ref, x1_ref, wg_ref, wu_ref, wd_ref, o_ref):
    step = pl.program_id(1)

    @pl.when(step == 0)
    def _():
        o_ref[...] = x1_ref[...]

    h = h_ref[...]
    gates = gates_ref[...]
    lane = lax.broadcasted_iota(jnp.int32, gates.shape, 1)
    acc = o_ref[...]
    for j in range(EXPERTS_PER_STEP):
        ge = jnp.sum(jnp.where(lane == step * EXPERTS_PER_STEP + j, gates, 0.0), axis=-1, keepdims=True)
        a = jax.nn.silu(_dot(h, wg_ref[j].astype(BF16))) * _dot(h, wu_ref[j].astype(BF16)) * ge
        acc = acc + _dot(a.astype(BF16), wd_ref[j].astype(BF16))
    o_ref[...] = acc


def _moe(h, xg, wg, wu, wd, layer):
    n = h.shape[0]
    tm = min(TM_MOE, n)
    return pl.pallas_call(
        _moe_kernel,
        grid=(n // tm, N_EXP // EXPERTS_PER_STEP),
        in_specs=[
            pl.BlockSpec((tm, D_MODEL), lambda i, e: (i, 0)),
            pl.BlockSpec((tm, ROUTER_LANES), lambda i, e: (i, D_MODEL // ROUTER_LANES)),
            pl.BlockSpec((tm, D_MODEL), lambda i, e: (i, 0)),
            pl.BlockSpec((None, EXPERTS_PER_STEP, D_MODEL, D_E), lambda i, e: (layer, e, 0, 0)),
            pl.BlockSpec((None, EXPERTS_PER_STEP, D_MODEL, D_E), lambda i, e: (layer, e, 0, 0)),
            pl.BlockSpec((None, EXPERTS_PER_STEP, D_E, D_MODEL), lambda i, e: (layer, e, 0, 0)),
        ],
        out_specs=pl.BlockSpec((tm, D_MODEL), lambda i, e: (i, 0)),
        out_shape=jax.ShapeDtypeStruct((n, D_MODEL), F32),
        compiler_params=_params(("parallel", "arbitrary")),
        name="moe",
    )(h, xg, xg, wg, wu, wd)


SUBLANES = 8


def _row_copies(n_rows, make_copy):
    def issue(k, carry):
        r0 = pl.multiple_of(k * SUBLANES, SUBLANES)
        for u in range(SUBLANES):
            make_copy(r0, u).start()
        return carry

    def drain(k, carry):
        r0 = pl.multiple_of(k * SUBLANES, SUBLANES)
        for u in range(SUBLANES):
            make_copy(r0, u).wait()
        return carry

    lax.fori_loop(0, n_rows // SUBLANES, issue, 0)
    lax.fori_loop(0, n_rows // SUBLANES, drain, 0)


def _dispatch_kernel(pos_ref, tile_end_ref, xg_ref, sorted_hbm, zero_ref, sem, zero_sem, *, tm):
    @pl.when(pl.program_id(0) == 0)
    def _():
        zero_ref[...] = jnp.zeros(zero_ref.shape, F32)
        for g in range(N_GROUPS):
            first = tile_end_ref[g - 1] if g > 0 else 0

            @pl.when(tile_end_ref[g] > first)
            def _():
                start = pl.multiple_of((tile_end_ref[g] - 1) * TM_SORT, TM_SORT)
                fill = pltpu.make_async_copy(zero_ref, sorted_hbm.at[pl.ds(start, TM_SORT), :], zero_sem)
                fill.start()
                fill.wait()

    base = pl.program_id(0) * tm
    _row_copies(tm, lambda r0, u: pltpu.make_async_copy(
        xg_ref.at[pl.ds(r0, SUBLANES), :].at[pl.ds(u, 1), :],
        sorted_hbm.at[pl.ds(pos_ref[base + r0 + u], 1), :], sem))


def _dispatch(pos, tile_end, xg, n_pad):
    n = xg.shape[0]
    tm = min(TM_MOVE, n)
    return pl.pallas_call(
        functools.partial(_dispatch_kernel, tm=tm),
        grid_spec=pltpu.PrefetchScalarGridSpec(
            num_scalar_prefetch=2,
            grid=(n // tm,),
            in_specs=[pl.BlockSpec((tm, XG_W), lambda i, pos, te: (i, 0))],
            out_specs=pl.BlockSpec(memory_space=pl.ANY),
            scratch_shapes=[pltpu.VMEM((TM_SORT, XG_W), F32), pltpu.SemaphoreType.DMA(()), pltpu.SemaphoreType.DMA(())],
        ),
        out_shape=jax.ShapeDtypeStruct((n_pad, XG_W), F32),
        compiler_params=_params(("arbitrary",)),
        name="moe_dispatch",
    )(pos, tile_end, xg)


def _combine_kernel(pos_ref, sorted_hbm, o_ref, sem, *, tm):
    base = pl.program_id(0) * tm
    _row_copies(tm, lambda r0, u: pltpu.make_async_copy(
        sorted_hbm.at[pl.ds(pos_ref[base + r0 + u], 1), :],
        o_ref.at[pl.ds(r0, SUBLANES), :].at[pl.ds(u, 1), :], sem))


def _combine(pos, y_sorted, n):
    tm = min(TM_MOVE, n)
    return pl.pallas_call(
        functools.partial(_combine_kernel, tm=tm),
        grid_spec=pltpu.PrefetchScalarGridSpec(
            num_scalar_prefetch=1,
            grid=(n // tm,),
            in_specs=[pl.BlockSpec(memory_space=pl.ANY)],
            out_specs=pl.BlockSpec((tm, D_MODEL), lambda i, pos: (i, 0)),
            scratch_shapes=[pltpu.SemaphoreType.DMA(())],
        ),
        out_shape=jax.ShapeDtypeStruct((n, D_MODEL), F32),
        compiler_params=_params(("arbitrary",)),
        name="moe_combine",
    )(pos, y_sorted)


def _moe_sorted_kernel(tg_ref, nv_ref, xg_ref, g2_ref, wg_ref, wu_ref, wd_ref, o_ref, wgb_ref, wub_ref, wdb_ref):
    j = pl.program_id(0)

    @pl.when(j >= nv_ref[0])
    def _():
        o_ref[...] = jnp.zeros(o_ref.shape, F32)

    @pl.when(j < nv_ref[0])
    def _():
        group = tg_ref[j]

        @pl.when((j == 0) | (group != tg_ref[jnp.maximum(j - 1, 0)]))
        def _():
            wgb_ref[...] = wg_ref[...].astype(BF16)
            wub_ref[...] = wu_ref[...].astype(BF16)
            wdb_ref[...] = wd_ref[...].astype(BF16)

        first_expert = group * E_PER
        x1 = xg_ref[:, :D_MODEL]
        gates = xg_ref[:, D_MODEL:]
        hb = (_rms(x1) * g2_ref[...]).astype(BF16)
        lane = lax.broadcasted_iota(jnp.int32, gates.shape, 1)
        acc = x1
        for e in range(E_PER):
            ge = jnp.sum(jnp.where(lane == first_expert + e, gates, 0.0), axis=-1, keepdims=True)
            a = jax.nn.silu(_dot(hb, wgb_ref[e])) * _dot(hb, wub_ref[e]) * ge
            acc = acc + _dot(a.astype(BF16), wdb_ref[e])
        o_ref[...] = acc


def _moe_sorted(tile_group, n_valid, xg_sorted, g2, wg, wu, wd, layer):
    n_pad = xg_sorted.shape[0]
    tm = TM_SORT
    by_group = lambda w: w.reshape(DEPTH, N_GROUPS, E_PER, *w.shape[2:])
    last = lambda j, nv: jnp.minimum(j, nv[0] - 1)
    w_spec = lambda a, b: pl.BlockSpec((None, None, E_PER, a, b), lambda j, tg, nv: (layer, tg[last(j, nv)], 0, 0, 0))
    return pl.pallas_call(
        _moe_sorted_kernel,
        grid_spec=pltpu.PrefetchScalarGridSpec(
            num_scalar_prefetch=2,
            grid=(n_pad // tm,),
            in_specs=[
                pl.BlockSpec((tm, XG_W), lambda j, tg, nv: (last(j, nv), 0)),
                pl.BlockSpec((1, D_MODEL), lambda j, tg, nv: (0, 0)),
                w_spec(D_MODEL, D_E), w_spec(D_MODEL, D_E), w_spec(D_E, D_MODEL),
            ],
            out_specs=pl.BlockSpec((tm, D_MODEL), lambda j, tg, nv: (j, 0)),
            scratch_shapes=[
                pltpu.VMEM((E_PER, D_MODEL, D_E), BF16),
                pltpu.VMEM((E_PER, D_MODEL, D_E), BF16),
                pltpu.VMEM((E_PER, D_E, D_MODEL), BF16),
            ],
        ),
        out_shape=jax.ShapeDtypeStruct((n_pad, D_MODEL), F32),
        compiler_params=_params(("arbitrary",), VMEM_LIMIT_SORTED_MOE_BYTES),
        name="moe_sorted",
    )(tile_group, n_valid, xg_sorted, g2, by_group(wg), by_group(wu), by_group(wd))


def _moe_grouped(xg, counts, g2, wg, wu, wd, layer):
    n = xg.shape[0]
    n_pad = n + N_GROUPS * TM_SORT
    n_tiles = n_pad // TM_SORT
    meta = xg[:, D_MODEL + META_GROUP_LANE:D_MODEL + META_RANK_LANE + 1].astype(jnp.int32)
    tiles_per_group = (counts[:N_GROUPS, 0].astype(jnp.int32) + TM_SORT - 1) // TM_SORT
    tile_end = jnp.cumsum(tiles_per_group)
    pos = ((tile_end - tiles_per_group) * TM_SORT)[meta[:, 0]] + meta[:, 1]
    tile_group = jnp.minimum(jnp.sum(jnp.arange(n_tiles)[:, None] >= tile_end[None, :], axis=1), N_GROUPS - 1)
    xg_sorted = _dispatch(pos, tile_end, xg, n_pad)
    y_sorted = _moe_sorted(tile_group.astype(jnp.int32), tile_end[N_GROUPS - 1:], xg_sorted, g2, wg, wu, wd, layer)
    return _combine(pos, y_sorted, n)


def _router_weights(r_g, r_e):
    r = jnp.concatenate([r_e, r_g, jnp.zeros((D_MODEL, ROUTER_LANES - N_EXP - N_GROUPS), F32)], axis=1)
    hi = r.astype(BF16)
    lo = (r - hi.astype(F32)).astype(BF16)
    return hi, lo


def kernel(x_prompt, x_sample, cache_k, cache_v, page_table, norm1_g, w_in, a_vnorm_g, a_ws, a_bs, qn_g, kn_g, lam_q1, lam_k1, lam_q2, lam_k2, b_onorm_g, w_out, norm2_g, router_g, router_e, w_gate, w_up, w_down):
    batch, seq, _ = x_prompt.shape
    dec_batch, n_new, _ = x_sample.shape
    xp = x_prompt.reshape(batch * seq, D_MODEL)
    xs = x_sample.reshape(dec_batch * n_new, D_MODEL)

    ii = jnp.arange(MXU_DIM) // DK
    bd = (ii[:, None] == ii[None, :]).astype(BF16)
    eye = jnp.eye(CHUNK // n_new, dtype=F32)

    w_in_b, w_out_b = w_in.astype(BF16), w_out.astype(BF16)
    wg, wu, wd = w_gate, w_up, w_down

    rows_p, rows_s = None, None
    for l in range(DEPTH):
        lam_init = 0.8 - 0.6 * math.exp(-0.3 * l)
        rhi, rlo = _router_weights(router_g[l], router_e[l])
        g1 = norm1_g[l][None]
        g2 = norm2_g[l][None]
        avg = a_vnorm_g[l][None]
        qg = jnp.tile(qn_g[l], D_QK // DK)[None]
        kg = jnp.tile(kn_g[l], D_QK // DK)[None]
        og = b_onorm_g[l][None]
        lams = (lam_q1[l][None], lam_k1[l][None], lam_q2[l][None], lam_k2[l][None])
        ws_p = a_ws[l]
        bias_p = jnp.repeat(a_bs[l].T, DH_A, axis=1)
        ws_s = jax.vmap(lambda w: jnp.kron(eye, w[:n_new, :n_new]))(a_ws[l])
        bias_s = jnp.tile(jnp.repeat(a_bs[l][:, :n_new].T, DH_A, axis=1), (CHUNK // n_new, 1))

        aout, q, k_all, kb, v_all, vb = _in_proj(xp, g1, w_in_b, avg, qg, kg, ws_p, bias_p, bd, l, rows_p, False)
        rows_p = (k_all, v_all)
        bout = _attn_prompt(q, kb, vb, *lams, og, lam_init, batch, seq)
        xg, counts = _merge_router(aout, bout, xp, w_out_b, l, g2, rhi, rlo, True)
        xp = _moe_grouped(xg, counts, g2, wg, wu, wd, l)

        aout, q, k_all, kb, v_all, vb, va_all = _in_proj(xs, g1, w_in_b, avg, qg, kg, ws_s, bias_s, bd, l, rows_s, True)
        rows_s = (k_all, v_all, va_all)
        bout = _attn_sample(q, kb, vb, cache_k, cache_v, page_table, l, *lams, og, lam_init, dec_batch, n_new)
        xg, h = _merge_router(aout, bout, xs, w_out_b, l, g2, rhi, rlo, False)
        xs = _moe(h, xg, wg, wu, wd, l)

    return (xp.reshape(batch, seq, D_MODEL), xs.reshape(dec_batch, n_new, D_MODEL),
            rows_p[0].reshape(DEPTH, batch, seq, H_B, 2 * DK), rows_p[1].reshape(DEPTH, batch, seq, H_B, DV),
            rows_s[0].reshape(DEPTH, dec_batch, n_new, H_B, 2 * DK), rows_s[1].reshape(DEPTH, dec_batch, n_new, H_B, DV),
            rows_s[2].reshape(DEPTH, dec_batch, n_new, D_A))
```
